```python
import jax, jax.numpy as jnp
from jax import lax
import numpy as np

D_MODEL = 1024
BATCH = 16
SEQ = 2048
DEPTH = 2

GRID_W = 64
CTX_LEN = 256
E_CONV = 768
CONV_WIDTH = 31
E_FOURIER = 512
FOURIER_GROUPS = 4
E_REC = 768
REC_HEADS = 6
REC_KEY = E_REC // REC_HEADS
REC_VAL = E_REC // REC_HEADS
CHUNK = 32
N_BRANCH = 3
EPS = 1e-6
K_MAX = 1.0 - 1e-6
_SPLIT_SIZES = (E_CONV, E_CONV, E_CONV,
                E_FOURIER, E_FOURIER,
                E_REC, E_REC, E_REC, E_REC, E_REC,
                N_BRANCH * D_MODEL)
D_IN = 3 * E_CONV + 2 * E_FOURIER + 5 * E_REC + N_BRANCH * D_MODEL

kernel_name = "hybrid_conv_fourier_hgrn2_dit_block"


def _rmsnorm(x, g):
    x32 = x.astype(jnp.float32)
    y = x32 * lax.rsqrt(jnp.mean(x32 * x32, axis=-1, keepdims=True) + EPS)
    return (y * g.astype(jnp.float32)).astype(x.dtype)


def _layernorm(x, g, b):
    x32 = x.astype(jnp.float32)
    mu = jnp.mean(x32, axis=-1, keepdims=True)
    xc = x32 - mu
    y = xc * lax.rsqrt(jnp.mean(xc * xc, axis=-1, keepdims=True) + EPS)
    return (y * g.astype(jnp.float32) + b.astype(jnp.float32)).astype(x.dtype)


def _sincos_2d(rows, cols, d):
    quarter = d // 4
    omega = 1.0 / (10000.0 ** (jnp.arange(quarter, dtype=jnp.float32) / quarter))
    er = jnp.arange(rows, dtype=jnp.float32)[:, None] * omega
    ec = jnp.arange(cols, dtype=jnp.float32)[:, None] * omega
    emb_r = jnp.concatenate([jnp.sin(er), jnp.cos(er)], axis=-1)
    emb_c = jnp.concatenate([jnp.sin(ec), jnp.cos(ec)], axis=-1)
    emb = jnp.concatenate([jnp.broadcast_to(emb_r[:, None, :], (rows, cols, d // 2)),
                           jnp.broadcast_to(emb_c[None, :, :], (rows, cols, d // 2))], axis=-1)
    return emb.reshape(rows * cols, d)


def _split_proj(p):
    out, idx = [], 0
    for s in _SPLIT_SIZES:
        out.append(p[..., idx:idx + s])
        idx += s
    return out


def _depthwise(u, w):
    pad = CONV_WIDTH // 2
    return lax.conv_general_dilated(u, w.astype(u.dtype), window_strides=(1,), padding=[(pad, pad)],
                                    dimension_numbers=('NWC', 'WIO', 'NWC'),
                                    feature_group_count=u.shape[-1])


def _fourier(u):
    b, t, e = u.shape
    ug = u.astype(jnp.float32).reshape(b, t, FOURIER_GROUPS, e // FOURIER_GROUPS)
    f = jnp.fft.fft2(ug, axes=(1, 3), norm="ortho").real
    return f.reshape(b, t, e).astype(u.dtype)


def _rec_inputs(q, f_logit, i, lb):
    b, t, _ = q.shape
    z = f_logit.astype(jnp.float32)
    k = jnp.minimum((1.0 - lb) * jax.nn.sigmoid(-z), K_MAX)
    logf = jnp.log1p(-k)
    shp = (b, t, REC_HEADS, REC_KEY)
    return (q.astype(jnp.float32).reshape(shp), k.reshape(shp),
            i.astype(jnp.float32).reshape(b, t, REC_HEADS, REC_VAL), logf.reshape(shp))


def _chunks(a):
    b, t, h, d = a.shape
    return a.reshape(b, t // CHUNK, CHUNK, h, d).transpose(1, 0, 3, 2, 4)


def _unchunk(a):
    n, b, h, l, d = a.shape
    return a.transpose(1, 0, 3, 2, 4).reshape(b, n * l, h, d)


def _chunk_scan(q, k, v, logf, s0):
    mask = jnp.tril(jnp.ones((CHUNK, CHUNK), dtype=bool))[:, :, None]

    def step(S, inp):
        qc, kc, vc, gc = inp
        bcum = jnp.cumsum(gc, axis=2)
        b_last = bcum[:, :, -1:, :]
        o_inter = jnp.einsum('bhlk,bhkv->bhlv', qc * jnp.exp(bcum), S)
        diff = bcum[:, :, :, None, :] - bcum[:, :, None, :, :]
        decay = jnp.where(mask, jnp.exp(jnp.where(mask, diff, 0.0)), 0.0)
        scores = jnp.einsum('bhtk,bhsk,bhtsk->bhts', qc, kc, decay)
        o_intra = jnp.einsum('bhts,bhsv->bhtv', scores, vc)
        k_dec = kc * jnp.exp(b_last - bcum)
        S_new = jnp.exp(b_last[:, :, 0, :, None]) * S + jnp.einsum('bhlk,bhlv->bhkv', k_dec, vc)
        return S_new, o_inter + o_intra

    s_fin, o = lax.scan(step, s0, (_chunks(q), _chunks(k), _chunks(v), _chunks(logf)))
    return _unchunk(o), s_fin


def _flip(a):
    return a[:, ::-1]


def _mix(parts, o_rec, conv_w, conv_b, ln_g, ln_b, rec_g, w_pa, w_pb, w_pc, w_out):
    a_val, a_gate, a_z, b_u, b_z, _q, _ff, _fb, _i, c_z, gates = parts
    u = a_val * jax.nn.sigmoid(a_gate)
    u = _depthwise(u, conv_w) + conv_b
    u = _layernorm(u, ln_g, ln_b)
    y_a = (jax.nn.silu(u) * jax.nn.silu(a_z)) @ w_pa
    y_b = (_fourier(b_u) * jax.nn.silu(b_z)) @ w_pb
    bsz, t = o_rec.shape[0], o_rec.shape[1]
    o = o_rec * lax.rsqrt(jnp.mean(o_rec * o_rec, axis=-1, keepdims=True) + EPS)
    o = (o.reshape(bsz, t, E_REC) * rec_g.astype(jnp.float32)).astype(c_z.dtype)
    y_c = (o * jax.nn.silu(c_z)) @ w_pc
    g_a, g_b, g_c = jnp.split(jax.nn.sigmoid(gates), N_BRANCH, axis=-1)
    y = g_a * y_a + g_b * y_b + g_c * y_c
    return y @ w_out


def setup_inputs(seed: int = 0) -> dict:
    key = jax.random.key(seed)
    ks = jax.random.split(key, 24)

    def nrm(k, shape, scale):
        return jax.random.normal(k, shape, jnp.float32) * scale

    return {
        "x": nrm(ks[0], (BATCH, SEQ, D_MODEL), 1.0),
        "c": nrm(ks[1], (BATCH, D_MODEL), 1.0),
        "ctx": nrm(ks[2], (BATCH, CTX_LEN, D_MODEL), 1.0),
        "c_ctx": nrm(ks[3], (D_MODEL,), 1.0),
        "w_ada": nrm(ks[4], (DEPTH, D_MODEL, 3 * D_MODEL), 0.5 * D_MODEL ** -0.5),
        "b_ada": nrm(ks[5], (DEPTH, 3 * D_MODEL), 0.02),
        "norm_g": 1.0 + nrm(ks[6], (DEPTH, D_MODEL), 0.02),
        "w_in": nrm(ks[7], (DEPTH, D_MODEL, D_IN), D_MODEL ** -0.5),
        "conv_w": nrm(ks[8], (DEPTH, CONV_WIDTH, 1, E_CONV), CONV_WIDTH ** -0.5),
        "conv_b": nrm(ks[9], (DEPTH, E_CONV), 0.02),
        "conv_ln_g": 1.0 + nrm(ks[10], (DEPTH, E_CONV), 0.02),
        "conv_ln_b": nrm(ks[11], (DEPTH, E_CONV), 0.02),
        "rec_lb": nrm(ks[12], (2, DEPTH, E_REC), 0.5),
        "rec_norm_g": 1.0 + nrm(ks[13], (DEPTH, E_REC), 0.02),
        "w_pa": nrm(ks[14], (DEPTH, E_CONV, D_MODEL), E_CONV ** -0.5),
        "w_pb": nrm(ks[15], (DEPTH, E_FOURIER, D_MODEL), E_FOURIER ** -0.5),
        "w_pc": nrm(ks[16], (DEPTH, E_REC, D_MODEL), E_REC ** -0.5),
        "w_out": nrm(ks[17], (DEPTH, D_MODEL, D_MODEL), D_MODEL ** -0.5),
        "final_g": 1.0 + nrm(ks[18], (D_MODEL,), 0.02),
    }


def reference(x, c, ctx, c_ctx, w_ada, b_ada, norm_g, w_in, conv_w, conv_b, conv_ln_g, conv_ln_b,
              rec_lb, rec_norm_g, w_pa, w_pb, w_pc, w_out, final_g):
    n_tok = x.shape[1]
    rows = n_tok // GRID_W
    x = x + _sincos_2d(rows, GRID_W, D_MODEL).astype(x.dtype)[None]
    lb_soft = jax.nn.softmax(rec_lb.astype(jnp.float32), axis=1)
    lbs = jnp.cumsum(lb_soft, axis=1) - lb_soft[:, :1]
    bsz = x.shape[0]
    s_zero = jnp.zeros((bsz, REC_HEADS, REC_KEY, REC_VAL), jnp.float32)

    for l in range(DEPTH):
        last = l == DEPTH - 1
        shift_x, scale_x, gate_x = jnp.split(jax.nn.silu(c) @ w_ada[l] + b_ada[l], 3, axis=-1)
        shift_c, scale_c, gate_c = jnp.split(jax.nn.silu(c_ctx) @ w_ada[l] + b_ada[l], 3, axis=-1)
        hx = _rmsnorm(x, norm_g[l]) * (1.0 + scale_x[:, None]) + shift_x[:, None]
        hc = _rmsnorm(ctx, norm_g[l]) * (1.0 + scale_c) + shift_c
        px = _split_proj(hx @ w_in[l])
        pc = _split_proj(hc @ w_in[l])

        lb_f, lb_b = lbs[0, l], lbs[1, l]
        qc_, kcf, vc_, lcf = _rec_inputs(pc[5], pc[6], pc[8], lb_f)
        _, kcb, _, lcb = _rec_inputs(pc[5], pc[7], pc[8], lb_b)
        qx_, kxf, vx_, lxf = _rec_inputs(px[5], px[6], px[8], lb_f)
        _, kxb, _, lxb = _rec_inputs(px[5], px[7], px[8], lb_b)

        o_cf, s_cf = _chunk_scan(qc_, kcf, vc_, lcf, s_zero)
        o_cb, s_cb = _chunk_scan(_flip(qc_), _flip(kcb), _flip(vc_), _flip(lcb), s_zero)
        o_xf, _ = _chunk_scan(qx_, kxf, vx_, lxf, s_cf)
        o_xb, _ = _chunk_scan(_flip(qx_), _flip(kxb), _flip(vx_), _flip(lxb), s_cb)
        o_x = o_xf + _flip(o_xb)

        y_x = _mix(px, o_x, conv_w[l], conv_b[l], conv_ln_g[l], conv_ln_b[l], rec_norm_g[l],
                   w_pa[l], w_pb[l], w_pc[l], w_out[l])
        x = x + gate_x[:, None] * y_x
        if not last:
            o_c = o_cf + _flip(o_cb)
            y_c = _mix(pc, o_c, conv_w[l], conv_b[l], conv_ln_g[l], conv_ln_b[l], rec_norm_g[l],
                       w_pa[l], w_pb[l], w_pc[l], w_out[l])
            ctx = ctx + gate_c * y_c

    return _rmsnorm(x, final_g)
```

```python
import functools

import numpy as np
import jax
import jax.numpy as jnp
from jax import lax
from jax.experimental import pallas as pl
from jax.experimental.pallas import tpu as pltpu

F32 = jnp.float32
BF16 = jnp.bfloat16

E_CONV = 768
CONV_WIDTH = 31
E_FOURIER = 512
FOURIER_GROUPS = 4
E_REC = 768
REC_HEADS = 6
HEAD = E_REC // REC_HEADS
N_BRANCH = 3
EPS = 1e-6
K_MAX = 1.0 - 1e-6
GRID_W = 64

CHUNK = 128
BASE = 4
SUBLANES = 8
HALO = 16
VMEM_LIMIT = 56 * 1024 * 1024


def _sigmoid(x):
    return 1.0 / (1.0 + jnp.exp(-x))


def _silu(x):
    return x * _sigmoid(x)


def _dot(a, b):
    return jnp.dot(a, b, preferred_element_type=F32)


def _dot_nt(a, b):
    return lax.dot_general(a, b, (((1,), (1,)), ((), ())), preferred_element_type=F32)


def _dot_tn(a, b):
    return lax.dot_general(a, b, (((0,), (0,)), ((), ())), preferred_element_type=F32)


def _params(*sem):
    return pltpu.CompilerParams(dimension_semantics=sem, vmem_limit_bytes=VMEM_LIMIT)


def _ada_kernel(cc_ref, w_ref, b_ref, o_ref):
    s = _silu(cc_ref[...])
    o_ref[0] = jnp.dot(s, w_ref[0], preferred_element_type=F32,
                       precision=lax.Precision.HIGHEST) + b_ref[0]


def _ada(cc, w_ada, b_ada):
    depth, d, d3 = w_ada.shape
    rows = cc.shape[0]
    return pl.pallas_call(
        _ada_kernel,
        grid=(depth, d3 // d),
        in_specs=[pl.BlockSpec((rows, d), lambda l, j: (0, 0)),
                  pl.BlockSpec((1, d, d), lambda l, j: (l, 0, j)),
                  pl.BlockSpec((1, 1, d), lambda l, j: (l, 0, j))],
        out_specs=pl.BlockSpec((1, rows, d), lambda l, j: (l, 0, j)),
        out_shape=jax.ShapeDtypeStruct((depth, rows, d3), F32),
        compiler_params=_params("arbitrary", "arbitrary"),
        name="ada",
    )(cc, w_ada, b_ada.reshape(depth, 1, d3))


_SEC = {}
_off = 0
for _name, _size in (("a_val", E_CONV), ("a_gate", E_CONV), ("a_z", E_CONV),
                     ("b_u", E_FOURIER), ("b_z", E_FOURIER),
                     ("q", E_REC), ("f_fwd", E_REC), ("f_bwd", E_REC), ("i", E_REC), ("c_z", E_REC)):
    _SEC[_name] = (_off, _off + _size)
    _off += _size
GATES_OFF = _off


def _inproj_kernel(has_pos, d_model, *refs):
    if has_pos:
        x_ref, pos_ref, refs = refs[0], refs[1], refs[2:]
    else:
        x_ref, pos_ref, refs = refs[0], None, refs[1:]
    (mult_ref, shift_ref, w_ref, omlb_ref,
     u_ref, saz_ref, bu_ref, sbz_ref, q_ref, kf_ref, kb_ref, lf_ref, lb_ref, v_ref, scz_ref, sg_ref) = refs

    x = x_ref[0]
    if has_pos:
        x = x + pos_ref[...]
    ms = jnp.mean(x * x, axis=-1, keepdims=True)
    h = (x * lax.rsqrt(ms + EPS) * mult_ref[0] + shift_ref[0]).astype(BF16)

    def proj(name):
        a, b = _SEC[name]
        return _dot(h, w_ref[:, a:b])

    u_ref[0] = (proj("a_val") * _sigmoid(proj("a_gate"))).astype(BF16)
    saz_ref[0] = _silu(proj("a_z")).astype(BF16)
    bu_ref[0] = proj("b_u").astype(BF16)
    sbz_ref[0] = _silu(proj("b_z")).astype(BF16)
    q_ref[0] = proj("q").astype(BF16)
    for name, k_ref, l_ref, row in (("f_fwd", kf_ref, lf_ref, 0), ("f_bwd", kb_ref, lb_ref, 1)):
        k = jnp.minimum(omlb_ref[row:row + 1, :] * _sigmoid(-proj(name)), K_MAX)
        k_ref[0] = k.astype(BF16)
        l_ref[0] = jnp.log1p(-k)
    v_ref[0] = proj("i").astype(BF16)
    scz_ref[0] = _silu(proj("c_z")).astype(BF16)
    for j in range(N_BRANCH):
        a = GATES_OFF + j * d_model
        sg_ref[0, :, j * d_model:(j + 1) * d_model] = _sigmoid(_dot(h, w_ref[:, a:a + d_model])).astype(BF16)


def _inproj(x, pos, mult, shift, w_bf16, omlb, tm):
    bsz, t, d = x.shape
    d_in = w_bf16.shape[1]
    has_pos = pos is not None
    tok = lambda w: pl.BlockSpec((1, tm, w), lambda b, i: (b, i, 0))
    in_specs = [tok(d)]
    args = [x]
    if has_pos:
        in_specs.append(pl.BlockSpec((tm, d), lambda b, i: (i, 0)))
        args.append(pos)
    in_specs += [pl.BlockSpec((1, 1, d), lambda b, i: (b, 0, 0)),
                 pl.BlockSpec((1, 1, d), lambda b, i: (b, 0, 0)),
                 pl.BlockSpec((d, d_in), lambda b, i: (0, 0), pipeline_mode=pl.Buffered(1)),
                 pl.BlockSpec((2, E_REC), lambda b, i: (0, 0))]
    args += [mult, shift, w_bf16, omlb]
    widths = [(E_CONV, BF16), (E_CONV, BF16), (E_FOURIER, BF16), (E_FOURIER, BF16),
              (E_REC, BF16), (E_REC, BF16), (E_REC, BF16), (E_REC, F32), (E_REC, F32),
              (E_REC, BF16), (E_REC, BF16), (N_BRANCH * d, BF16)]
    return pl.pallas_call(
        functools.partial(_inproj_kernel, has_pos, d),
        grid=(bsz, t // tm),
        in_specs=in_specs,
        out_specs=[tok(w) for w, _ in widths],
        out_shape=[jax.ShapeDtypeStruct((bsz, t, w), dt) for w, dt in widths],
        compiler_params=_params("parallel", "parallel"),
        name="inproj",
    )(*args)


def _fourier_kernel(t, scale, u_ref, gate_ref, cs_ref, cc_ref, o_ref, ab_scr):
    gw = E_FOURIER // FOURIER_GROUPS

    @pl.when(pl.program_id(1) == 0)
    def _():
        for g in range(FOURIER_GROUPS):
            r = _dot(u_ref[0, :, g * gw:(g + 1) * gw], cc_ref[...])
            ab_scr[0:t, g * gw:(g + 1) * gw] = r[:, :gw].astype(BF16)
            ab_scr[t:2 * t, g * gw:(g + 1) * gw] = r[:, gw:].astype(BF16)

    f = _dot(cs_ref[...], ab_scr[...])
    o_ref[0] = (f * scale * gate_ref[0].astype(F32)).astype(BF16)


def _dft_consts(t):
    gw = E_FOURIER // FOURIER_GROUPS
    jk = np.outer(np.arange(t), np.arange(t)) % t
    ang = 2.0 * np.pi * jk / t
    cs = np.concatenate([np.cos(ang), -np.sin(ang)], axis=1)
    jc = np.outer(np.arange(gw), np.arange(gw)) % gw
    angc = 2.0 * np.pi * jc / gw
    cc = np.concatenate([np.cos(angc), np.sin(angc)], axis=1)
    scale = 1.0 / np.sqrt(float(t) * gw)
    return jnp.asarray(cs, dtype=BF16), jnp.asarray(cc, dtype=BF16), float(scale)


def _fourier(bu, sbz, tr):
    bsz, t, e = bu.shape
    gw = e // FOURIER_GROUPS
    cs, cc, scale = _dft_consts(t)
    return pl.pallas_call(
        functools.partial(_fourier_kernel, t, scale),
        grid=(bsz, t // tr),
        in_specs=[pl.BlockSpec((1, t, e), lambda b, i: (b, 0, 0)),
                  pl.BlockSpec((1, tr, e), lambda b, i: (b, i, 0)),
                  pl.BlockSpec((tr, 2 * t), lambda b, i: (i, 0)),
                  pl.BlockSpec((gw, 2 * gw), lambda b, i: (0, 0))],
        out_specs=pl.BlockSpec((1, tr, e), lambda b, i: (b, i, 0)),
        out_shape=jax.ShapeDtypeStruct((bsz, t, e), BF16),
        scratch_shapes=[pltpu.VMEM((2 * t, e), BF16)],
        compiler_params=_params("parallel", "arbitrary"),
        name="fourier",
    )(bu, sbz, cs, cc)


def _level_ids():
    t = np.arange(CHUNK)[:, None]
    s = np.arange(CHUNK)[None, :]
    lv = np.full((CHUNK, CHUNK), -1, np.int32)
    size, level = CHUNK, int(np.log2(CHUNK // BASE))
    while size >= BASE:
        lv = np.where((t // size == s // size) & (s <= t), level, lv)
        size //= 2
        level -= 1
    return lv.astype(np.int32)


N_LEVELS = int(np.log2(CHUNK // BASE)) + 1


def _row(ref, r):
    return jnp.broadcast_to(ref[pl.ds(r, 1), :], (SUBLANES, HEAD))


def _level_factors(level, bc, bc_ref, q, k, fwd):
    groups = CHUNK // SUBLANES
    sub = lax.broadcasted_iota(jnp.int32, (CHUNK, HEAD), 0) % SUBLANES
    if level == 0:
        lo_row, hi_row = (0, BASE) if fwd else (BASE - 1, SUBLANES - 1)
        ref = jnp.concatenate(
            [jnp.where(sub[:SUBLANES] < BASE, _row(bc_ref, g * SUBLANES + lo_row),
                       _row(bc_ref, g * SUBLANES + hi_row)) for g in range(groups)], axis=0)
        e = bc - ref
        return (q * jnp.exp(e)).astype(BF16), (k * jnp.exp(-e)).astype(BF16)
    half = BASE << (level - 1)
    if half < SUBLANES:
        ref_row = half - 1 if fwd else half
        ref = jnp.concatenate([_row(bc_ref, g * SUBLANES + ref_row) for g in range(groups)], axis=0)
        q_rows = (sub >= half) if fwd else (sub < half)
        ex = jnp.exp(jnp.where(q_rows, bc - ref, ref - bc))
        zero = jnp.zeros_like(ex)
        return (jnp.where(q_rows, q * ex, zero).astype(BF16),
                jnp.where(q_rows, zero, k * ex).astype(BF16))
    qs, ks = [], []
    zero = jnp.zeros((half, HEAD), F32)
    for blk in range(CHUNK // (2 * half)):
        a = blk * 2 * half
        lo, hi = slice(a, a + half), slice(a + half, a + 2 * half)
        ref = jnp.broadcast_to(bc_ref[pl.ds(a + half - 1 if fwd else a + half, 1), :], (half, HEAD))
        if fwd:
            qs += [zero, q[hi] * jnp.exp(bc[hi] - ref)]
            ks += [k[lo] * jnp.exp(ref - bc[lo]), zero]
        else:
            qs += [q[lo] * jnp.exp(bc[lo] - ref), zero]
            ks += [zero, k[hi] * jnp.exp(ref - bc[hi])]
    return jnp.concatenate(qs, axis=0).astype(BF16), jnp.concatenate(ks, axis=0).astype(BF16)


def _split3(g):
    hi = g.astype(BF16)
    r1 = g - hi.astype(F32)
    mid = r1.astype(BF16)
    lo = (r1 - mid.astype(F32)).astype(BF16)
    return jnp.concatenate([hi, mid, lo], axis=1)


def _rec_kernel(n_ctx, n_lat, ctx_out, *refs):
    (qx, kfx, kbx, lfx, lbx, vx, zx, qc, kfc, kbc, lfc, lbc, vc, zc,
     tril_ref, triu_ref, lvf_ref, lvb_ref, g_ref) = refs[:19]
    if ctx_out:
        ox_ref, oc_ref = refs[19:21]
        scr = refs[21:]
    else:
        ox_ref, oc_ref = refs[19], None
        scr = refs[20:]
    bcf_scr, bcb_scr, qfb_scr, oin_scr, dst_scr, a_scr, st_scr = scr
    kw = HEAD

    def chunk_a(q_ref, kf_ref, kb_ref, lf_ref, lb_ref, v_ref, c, slot):
        rows = pl.ds(pl.multiple_of(c * CHUNK, CHUNK), CHUNK)
        q = q_ref[0, rows, :].astype(F32)
        v = v_ref[0, rows, :]
        acc = []
        qi, kd, dec = [], [], []
        for fwd, k_ref, l_ref, tri_ref, lv_ref, bc_scr in (
                (True, kf_ref, lf_ref, tril_ref, lvf_ref, bcf_scr),
                (False, kb_ref, lb_ref, triu_ref, lvb_ref, bcb_scr)):
            k = k_ref[0, rows, :].astype(F32)
            cs = _dot(tri_ref[...], _split3(l_ref[0, rows, :]))
            bc = cs[:, :kw] + cs[:, kw:2 * kw] + cs[:, 2 * kw:]
            bc_scr[...] = bc
            edge = CHUNK - 1 if fwd else 0
            b_edge = bc_scr[pl.ds(edge, 1), :]
            qi.append((q * jnp.exp(bc)).astype(BF16))
            kd.append((k * jnp.exp(b_edge - bc)).astype(BF16))
            dec.append(jnp.exp(b_edge))
            lv = lv_ref[...]
            sc = jnp.zeros((CHUNK, CHUNK), F32)
            for level in range(N_LEVELS):
                qt, kt = _level_factors(level, bc, bc_scr, q, k, fwd)
                sc = jnp.where(lv == level, _dot_nt(qt, kt), sc)
            acc.append(sc)
        scores = (acc[0] + acc[1]).astype(BF16)
        oin_scr[slot] = _dot(scores, v)
        qfb_scr[slot] = jnp.concatenate(qi, axis=1)
        dst_scr[slot] = _dot_tn(v, jnp.concatenate(kd, axis=1))
        a_scr[slot] = jnp.concatenate(dec, axis=1)

    def chunk_c(z_ref, o_ref, c, slot):
        rows = pl.ds(pl.multiple_of(c * CHUNK, CHUNK), CHUNK)
        o = oin_scr[slot] + _dot_nt(qfb_scr[slot], st_scr[slot])
        o = o * lax.rsqrt(jnp.mean(o * o, axis=-1, keepdims=True) + EPS)
        o_ref[0, rows, :] = (o * g_ref[...] * z_ref[0, rows, :].astype(F32)).astype(BF16)

    def loop_a_ctx(c, carry):
        chunk_a(qc, kfc, kbc, lfc, lbc, vc, c, c)
        return carry

    def loop_a_lat(c, carry):
        chunk_a(qx, kfx, kbx, lfx, lbx, vx, c, c + n_ctx)
        return carry

    lax.fori_loop(0, n_ctx, loop_a_ctx, 0)
    lax.fori_loop(0, n_lat, loop_a_lat, 0)

    n_all = n_ctx + n_lat
    fwd_order = list(range(n_all))
    bwd_order = list(range(n_ctx - 1, -1, -1)) + list(range(n_all - 1, n_ctx - 1, -1))
    for order, cols in ((fwd_order, slice(0, kw)), (bwd_order, slice(kw, 2 * kw))):
        s = jnp.zeros((kw, kw), F32)
        for slot in order:
            st_scr[slot, :, cols] = s.astype(BF16)
            s = a_scr[slot, :, cols] * s + dst_scr[slot, :, cols]

    def loop_c_lat(c, carry):
        chunk_c(zx, ox_ref, c, c + n_ctx)
        return carry

    lax.fori_loop(0, n_lat, loop_c_lat, 0)
    if ctx_out:
        def loop_c_ctx(c, carry):
            chunk_c(zc, oc_ref, c, c)
            return carry

        lax.fori_loop(0, n_ctx, loop_c_ctx, 0)


def _recurrence(lat, ctx, rec_g, ctx_out):
    bsz, t, _ = lat[0].shape
    tc = ctx[0].shape[1]
    n_lat, n_ctx = t // CHUNK, tc // CHUNK
    n_all = n_lat + n_ctx
    lvf = _level_ids()
    tri = np.tril(np.ones((CHUNK, CHUNK), np.float32))
    consts = [jnp.asarray(tri, dtype=BF16), jnp.asarray(tri.T, dtype=BF16),
              jnp.asarray(lvf), jnp.asarray(lvf.T)]
    head = lambda n: pl.BlockSpec((1, n, HEAD), lambda b, h: (b, 0, h))
    const = pl.BlockSpec((CHUNK, CHUNK), lambda b, h: (0, 0))
    in_specs = ([head(t)] * 7 + [head(tc)] * 7 + [const] * 4
                + [pl.BlockSpec((1, HEAD), lambda b, h: (0, h))])
    out_specs = [head(t)]
    out_shape = [jax.ShapeDtypeStruct((bsz, t, E_REC), BF16)]
    if ctx_out:
        out_specs.append(head(tc))
        out_shape.append(jax.ShapeDtypeStruct((bsz, tc, E_REC), BF16))
    scratch = [pltpu.VMEM((CHUNK, HEAD), F32), pltpu.VMEM((CHUNK, HEAD), F32),
               pltpu.VMEM((n_all, CHUNK, 2 * HEAD), BF16),
               pltpu.VMEM((n_all, CHUNK, HEAD), F32),
               pltpu.VMEM((n_all, HEAD, 2 * HEAD), F32),
               pltpu.VMEM((n_all, 1, 2 * HEAD), F32),
               pltpu.VMEM((n_all, HEAD, 2 * HEAD), BF16)]
    out = pl.pallas_call(
        functools.partial(_rec_kernel, n_ctx, n_lat, ctx_out),
        grid=(bsz, REC_HEADS),
        in_specs=in_specs,
        out_specs=out_specs,
        out_shape=out_shape,
        scratch_shapes=scratch,
        compiler_params=_params("parallel", "parallel"),
        name="recurrence",
    )(*lat, *ctx, *consts, rec_g.reshape(1, E_REC))
    return out if ctx_out else (out[0], None)


ROWS = 32


def _mix_kernel(has_pos, last, tm, n_tiles, d_model, *refs):
    refs = list(refs)
    u_ref, up_ref, un_ref, saz_ref, fb_ref, oc_ref, sg_ref, x_ref = refs[:8]
    refs = refs[8:]
    pos_ref = refs.pop(0) if has_pos else None
    gate_ref, cw_ref, cb_ref, lg_ref, lb_ref, wpa_ref, wpb_ref, wpc_ref, wo_ref = refs[:9]
    refs = refs[9:]
    fg_ref = refs.pop(0) if last else None
    o_ref, ush_scr, a_scr = refs

    i = pl.program_id(1)
    prev = jnp.where(i > 0, up_ref[0].astype(F32), 0.0)
    nxt = jnp.where(i < n_tiles - 1, un_ref[0].astype(F32), 0.0)
    ush_scr[0, 0:HALO, :] = prev
    ush_scr[0, HALO:HALO + tm, :] = u_ref[0].astype(F32)
    ush_scr[0, HALO + tm:, :] = nxt
    span = tm + 2 * HALO - SUBLANES
    for r in range(1, SUBLANES):
        ush_scr[r, 0:span, :] = ush_scr[0, r:r + span, :]

    def rows_body(rb, carry):
        base = pl.multiple_of(rb * ROWS, ROWS)
        acc = jnp.broadcast_to(cb_ref[...], (ROWS, E_CONV))
        for j in range(CONV_WIDTH):
            off = j + HALO - CONV_WIDTH // 2
            acc = acc + cw_ref[pl.ds(j, 1), :] * ush_scr[off % SUBLANES,
                                                         pl.ds(base + (off // SUBLANES) * SUBLANES, ROWS), :]
        mu = jnp.mean(acc, axis=-1, keepdims=True)
        xc = acc - mu
        var = jnp.mean(xc * xc, axis=-1, keepdims=True)
        y = xc * lax.rsqrt(var + EPS) * lg_ref[...] + lb_ref[...]
        a_scr[pl.ds(base, ROWS), :] = (_silu(y) * saz_ref[0, pl.ds(base, ROWS), :].astype(F32)).astype(BF16)
        return carry

    lax.fori_loop(0, tm // ROWS, rows_body, 0)

    ya = _dot(a_scr[...], wpa_ref[...])
    yb = _dot(fb_ref[0], wpb_ref[...])
    yc = _dot(oc_ref[0], wpc_ref[...])
    d = d_model
    y = (sg_ref[0, :, 0:d].astype(F32) * ya + sg_ref[0, :, d:2 * d].astype(F32) * yb
         + sg_ref[0, :, 2 * d:3 * d].astype(F32) * yc)
    z = _dot(y.astype(BF16), wo_ref[...])
    x = x_ref[0]
    if has_pos:
        x = x + pos_ref[...]
    xn = x + gate_ref[0] * z
    if last:
        xn = xn * lax.rsqrt(jnp.mean(xn * xn, axis=-1, keepdims=True) + EPS) * fg_ref[...]
    o_ref[0] = xn


def _mix(u, saz, fb, oc, sg, x, pos, gate, cw, cb, lg, lb, wpa, wpb, wpc, wo, final_g, tm):
    bsz, t, d = x.shape
    n_tiles = t // tm
    has_pos = pos is not None
    last = final_g is not None
    hb = tm // HALO
    n_hb = t // HALO
    tok = lambda w: pl.BlockSpec((1, tm, w), lambda b, i: (b, i, 0))
    full = lambda a: pl.BlockSpec(a.shape, lambda b, i: (0,) * a.ndim)
    in_specs = [tok(E_CONV),
                pl.BlockSpec((1, HALO, E_CONV), lambda b, i: (b, jnp.maximum(i * hb - 1, 0), 0)),
                pl.BlockSpec((1, HALO, E_CONV), lambda b, i: (b, jnp.minimum((i + 1) * hb, n_hb - 1), 0)),
                tok(E_CONV), tok(E_FOURIER), tok(E_REC), tok(N_BRANCH * d), tok(d)]
    args = [u, u, u, saz, fb, oc, sg, x]
    if has_pos:
        in_specs.append(pl.BlockSpec((tm, d), lambda b, i: (i, 0)))
        args.append(pos)
    params = [cw, cb, lg, lb, wpa, wpb, wpc, wo]
    in_specs += [pl.BlockSpec((1, 1, d), lambda b, i: (b, 0, 0))] + [full(a) for a in params]
    args += [gate] + params
    if last:
        in_specs.append(full(final_g))
        args.append(final_g)
    return pl.pallas_call(
        functools.partial(_mix_kernel, has_pos, last, tm, n_tiles, d),
        grid=(bsz, n_tiles),
        in_specs=in_specs,
        out_specs=tok(d),
        out_shape=jax.ShapeDtypeStruct((bsz, t, d), F32),
        scratch_shapes=[pltpu.VMEM((SUBLANES, tm + 2 * HALO, E_CONV), F32),
                        pltpu.VMEM((tm, E_CONV), BF16)],
        compiler_params=_params("parallel", "parallel"),
        name="mix",
    )(*args)


def _sincos_2d(rows, cols, d):
    quarter = d // 4
    omega = 1.0 / (10000.0 ** (jnp.arange(quarter, dtype=F32) / quarter))
    er = jnp.arange(rows, dtype=F32)[:, None] * omega
    ec = jnp.arange(cols, dtype=F32)[:, None] * omega
    emb_r = jnp.concatenate([jnp.sin(er), jnp.cos(er)], axis=-1)
    emb_c = jnp.concatenate([jnp.sin(ec), jnp.cos(ec)], axis=-1)
    emb = jnp.concatenate([jnp.broadcast_to(emb_r[:, None, :], (rows, cols, d // 2)),
                           jnp.broadcast_to(emb_c[None, :, :], (rows, cols, d // 2))], axis=-1)
    return emb.reshape(rows * cols, d)


def _tile(n, want):
    return want if n % want == 0 else n


def kernel(x, c, ctx, c_ctx, w_ada, b_ada, norm_g, w_in, conv_w, conv_b, conv_ln_g, conv_ln_b,
           rec_lb, rec_norm_g, w_pa, w_pb, w_pc, w_out, final_g):
    bsz, t, d = x.shape
    tc = ctx.shape[1]
    depth = w_in.shape[0]
    assert t % CHUNK == 0 and tc % CHUNK == 0 and t % GRID_W == 0

    pos = _sincos_2d(t // GRID_W, GRID_W, d)
    lb_soft = jax.nn.softmax(rec_lb.astype(F32), axis=1)
    lbs = jnp.cumsum(lb_soft, axis=1) - lb_soft[:, :1]

    rows = -(-(bsz + 1) // SUBLANES) * SUBLANES
    cc = jnp.zeros((rows, d), F32).at[:bsz].set(c).at[bsz].set(c_ctx)
    mod = _ada(cc, w_ada, b_ada)

    tm_x, tm_c = _tile(t, 256), _tile(tc, 256)
    for l in range(depth):
        last = l == depth - 1
        shift, scale, gate = mod[l, :, :d], mod[l, :, d:2 * d], mod[l, :, 2 * d:]
        mult = norm_g[l][None, :] * (1.0 + scale)
        mult_x, shift_x, gate_x = (a[:bsz, None, :] for a in (mult, shift, gate))
        mult_c, shift_c, gate_c = (jnp.broadcast_to(a[bsz][None, None, :], (bsz, 1, d))
                                   for a in (mult, shift, gate))
        w_l = w_in[l].astype(BF16)
        omlb = 1.0 - lbs[:, l, :]

        px = _inproj(x, pos if l == 0 else None, mult_x, shift_x, w_l, omlb, tm_x)
        pc = _inproj(ctx, None, mult_c, shift_c, w_l, omlb, tm_c)

        ox, oc = _recurrence(px[4:11], pc[4:11], rec_norm_g[l], ctx_out=not last)

        mixw = (conv_w[l].reshape(CONV_WIDTH, E_CONV), conv_b[l][None], conv_ln_g[l][None],
                conv_ln_b[l][None], w_pa[l].astype(BF16), w_pb[l].astype(BF16),
                w_pc[l].astype(BF16), w_out[l].astype(BF16))
        fbx = _fourier(px[2], px[3], _tile(t, 512))
        x = _mix(px[0], px[1], fbx, ox, px[11], x, pos if l == 0 else None, gate_x, *mixw,
                 final_g[None] if last else None, tm_x)
        if not last:
            fbc = _fourier(pc[2], pc[3], _tile(tc, 512))
            ctx = _mix(pc[0], pc[1], fbc, oc, pc[11], ctx, None, gate_c, *mixw, None, tm_c)
    return x
```

```python
import functools

import numpy as np
import jax
import jax.numpy as jnp
from jax import lax
from jax.experimental import pallas as pl
from jax.experimental.pallas import tpu as pltpu

F32 = jnp.float32
BF16 = jnp.bfloat16

E_CONV = 768
CONV_WIDTH = 31
E_FOURIER = 512
FOURIER_GROUPS = 4
E_REC = 768
REC_HEADS = 6
HEAD = E_REC // REC_HEADS
N_BRANCH = 3
EPS = 1e-6
K_MAX = 1.0 - 1e-6
GRID_W = 64

CHUNK = 128
BASE = 4
SUBLANES = 8
HALO = 16
VMEM_LIMIT = 56 * 1024 * 1024


def _sigmoid(x):
    return 1.0 / (1.0 + jnp.exp(-x))


def _silu(x):
    return x * _sigmoid(x)


def _dot(a, b):
    return jnp.dot(a, b, preferred_element_type=F32)


def _dot_nt(a, b):
    return lax.dot_general(a, b, (((1,), (1,)), ((), ())), preferred_element_type=F32)


def _dot_tn(a, b):
    return lax.dot_general(a, b, (((0,), (0,)), ((), ())), preferred_element_type=F32)


def _params(*sem):
    return pltpu.CompilerParams(dimension_semantics=sem, vmem_limit_bytes=VMEM_LIMIT)


def _ada_kernel(cc_ref, w_ref, b_ref, o_ref):
    s = _silu(cc_ref[...])
    o_ref[0] = jnp.dot(s, w_ref[0], preferred_element_type=F32,
                       precision=lax.Precision.HIGHEST) + b_ref[0]


def _ada(cc, w_ada, b_ada):
    depth, d, d3 = w_ada.shape
    rows = cc.shape[0]
    return pl.pallas_call(
        _ada_kernel,
        grid=(depth, d3 // d),
        in_specs=[pl.BlockSpec((rows, d), lambda l, j: (0, 0)),
                  pl.BlockSpec((1, d, d), lambda l, j: (l, 0, j)),
                  pl.BlockSpec((1, 1, d), lambda l, j: (l, 0, j))],
        out_specs=pl.BlockSpec((1, rows, d), lambda l, j: (l, 0, j)),
        out_shape=jax.ShapeDtypeStruct((depth, rows, d3), F32),
        compiler_params=_params("arbitrary", "arbitrary"),
        name="ada",
    )(cc, w_ada, b_ada.reshape(depth, 1, d3))


_SEC = {}
_off = 0
for _name, _size in (("a_val", E_CONV), ("a_gate", E_CONV), ("a_z", E_CONV),
                     ("b_u", E_FOURIER), ("b_z", E_FOURIER),
                     ("q", E_REC), ("f_fwd", E_REC), ("f_bwd", E_REC), ("i", E_REC), ("c_z", E_REC)):
    _SEC[_name] = (_off, _off + _size)
    _off += _size
GATES_OFF = _off


def _inproj_kernel(has_pos, d_model, *refs):
    if has_pos:
        x_ref, pos_ref, refs = refs[0], refs[1], refs[2:]
    else:
        x_ref, pos_ref, refs = refs[0], None, refs[1:]
    (mult_ref, shift_ref, w_ref, omlb_ref,
     u_ref, saz_ref, bu_ref, sbz_ref, q_ref, kf_ref, kb_ref, lf_ref, lb_ref, v_ref, scz_ref, sg_ref) = refs

    x = x_ref[0]
    if has_pos:
        x = x + pos_ref[...]
    ms = jnp.mean(x * x, axis=-1, keepdims=True)
    h = (x * lax.rsqrt(ms + EPS) * mult_ref[0] + shift_ref[0]).astype(BF16)

    def proj(name):
        a, b = _SEC[name]
        return _dot(h, w_ref[:, a:b])

    u_ref[0] = (proj("a_val") * _sigmoid(proj("a_gate"))).astype(BF16)
    saz_ref[0] = _silu(proj("a_z")).astype(BF16)
    bu_ref[0] = proj("b_u").astype(BF16)
    sbz_ref[0] = _silu(proj("b_z")).astype(BF16)
    q_ref[0] = proj("q").astype(BF16)
    for name, k_ref, l_ref, row in (("f_fwd", kf_ref, lf_ref, 0), ("f_bwd", kb_ref, lb_ref, 1)):
        k = jnp.minimum(omlb_ref[row:row + 1, :] * _sigmoid(-proj(name)), K_MAX)
        k_ref[0] = k.astype(BF16)
        l_ref[0] = jnp.log1p(-k)
    v_ref[0] = proj("i").astype(BF16)
    scz_ref[0] = _silu(proj("c_z")).astype(BF16)
    for j in range(N_BRANCH):
        a = GATES_OFF + j * d_model
        sg_ref[0, :, j * d_model:(j + 1) * d_model] = _sigmoid(_dot(h, w_ref[:, a:a + d_model])).astype(BF16)


def _inproj(x, pos, mult, shift, w_bf16, omlb, tm):
    bsz, t, d = x.shape
    d_in = w_bf16.shape[1]
    has_pos = pos is not None
    tok = lambda w: pl.BlockSpec((1, tm, w), lambda b, i: (b, i, 0))
    in_specs = [tok(d)]
    args = [x]
    if has_pos:
        in_specs.append(pl.BlockSpec((tm, d), lambda b, i: (i, 0)))
        args.append(pos)
    in_specs += [pl.BlockSpec((1, 1, d), lambda b, i: (b, 0, 0)),
                 pl.BlockSpec((1, 1, d), lambda b, i: (b, 0, 0)),
                 pl.BlockSpec((d, d_in), lambda b, i: (0, 0), pipeline_mode=pl.Buffered(1)),
                 pl.BlockSpec((2, E_REC), lambda b, i: (0, 0))]
    args += [mult, shift, w_bf16, omlb]
    widths = [(E_CONV, BF16), (E_CONV, BF16), (E_FOURIER, BF16), (E_FOURIER, BF16),
              (E_REC, BF16), (E_REC, BF16), (E_REC, BF16), (E_REC, F32), (E_REC, F32),
              (E_REC, BF16), (E_REC, BF16), (N_BRANCH * d, BF16)]
    return pl.pallas_call(
        functools.partial(_inproj_kernel, has_pos, d),
        grid=(bsz, t // tm),
        in_specs=in_specs,
        out_specs=[tok(w) for w, _ in widths],
        out_shape=[jax.ShapeDtypeStruct((bsz, t, w), dt) for w, dt in widths],
        compiler_params=_params("parallel", "parallel"),
        name="inproj",
    )(*args)


def _fourier_kernel(t, scale, u_ref, gate_ref, cs_ref, cc_ref, o_ref, ab_scr):
    gw = E_FOURIER // FOURIER_GROUPS

    @pl.when(pl.program_id(1) == 0)
    def _():
        for g in range(FOURIER_GROUPS):
            r = _dot(u_ref[0, :, g * gw:(g + 1) * gw], cc_ref[...])
            ab_scr[0:t, g * gw:(g + 1) * gw] = r[:, :gw].astype(BF16)
            ab_scr[t:2 * t, g * gw:(g + 1) * gw] = r[:, gw:].astype(BF16)

    f = _dot(cs_ref[...], ab_scr[...])
    o_ref[0] = (f * scale * gate_ref[0].astype(F32)).astype(BF16)


def _dft_consts(t):
    gw = E_FOURIER // FOURIER_GROUPS
    jk = np.outer(np.arange(t), np.arange(t)) % t
    ang = 2.0 * np.pi * jk / t
    cs = np.concatenate([np.cos(ang), -np.sin(ang)], axis=1)
    jc = np.outer(np.arange(gw), np.arange(gw)) % gw
    angc = 2.0 * np.pi * jc / gw
    cc = np.concatenate([np.cos(angc), np.sin(angc)], axis=1)
    scale = 1.0 / np.sqrt(float(t) * gw)
    return jnp.asarray(cs, dtype=BF16), jnp.asarray(cc, dtype=BF16), float(scale)


def _fourier(bu, sbz, tr):
    bsz, t, e = bu.shape
    gw = e // FOURIER_GROUPS
    cs, cc, scale = _dft_consts(t)
    return pl.pallas_call(
        functools.partial(_fourier_kernel, t, scale),
        grid=(bsz, t // tr),
        in_specs=[pl.BlockSpec((1, t, e), lambda b, i: (b, 0, 0)),
                  pl.BlockSpec((1, tr, e), lambda b, i: (b, i, 0)),
                  pl.BlockSpec((tr, 2 * t), lambda b, i: (i, 0)),
                  pl.BlockSpec((gw, 2 * gw), lambda b, i: (0, 0))],
        out_specs=pl.BlockSpec((1, tr, e), lambda b, i: (b, i, 0)),
        out_shape=jax.ShapeDtypeStruct((bsz, t, e), BF16),
        scratch_shapes=[pltpu.VMEM((2 * t, e), BF16)],
        compiler_params=_params("parallel", "arbitrary"),
        name="fourier",
    )(bu, sbz, cs, cc)


def _level_ids():
    t = np.arange(CHUNK)[:, None]
    s = np.arange(CHUNK)[None, :]
    lv = np.full((CHUNK, CHUNK), -1, np.int32)
    size, level = CHUNK, int(np.log2(CHUNK // BASE))
    while size >= BASE:
        lv = np.where((t // size == s // size) & (s <= t), level, lv)
        size //= 2
        level -= 1
    return lv.astype(np.int32)


N_LEVELS = int(np.log2(CHUNK // BASE)) + 1
A_UNROLL = 4
C_UNROLL = 4


def _row(ref, r):
    return jnp.broadcast_to(ref[pl.ds(r, 1), :], (SUBLANES, HEAD))


def _level_factors(level, bc, bc_ref, q, k, fwd):
    groups = CHUNK // SUBLANES
    sub = lax.broadcasted_iota(jnp.int32, (CHUNK, HEAD), 0) % SUBLANES
    if level == 0:
        lo_row, hi_row = (0, BASE) if fwd else (BASE - 1, SUBLANES - 1)
        ref = jnp.concatenate(
            [jnp.where(sub[:SUBLANES] < BASE, _row(bc_ref, g * SUBLANES + lo_row),
                       _row(bc_ref, g * SUBLANES + hi_row)) for g in range(groups)], axis=0)
        e = bc - ref
        return (q * jnp.exp(e)).astype(BF16), (k * jnp.exp(-e)).astype(BF16), [slice(0, CHUNK)]
    half = BASE << (level - 1)
    if half < SUBLANES:
        ref_row = half - 1 if fwd else half
        ref = jnp.concatenate([_row(bc_ref, g * SUBLANES + ref_row) for g in range(groups)], axis=0)
        q_rows = (sub >= half) if fwd else (sub < half)
        ex = jnp.exp(jnp.where(q_rows, bc - ref, ref - bc))
        zero = jnp.zeros_like(ex)
        return (jnp.where(q_rows, q * ex, zero).astype(BF16),
                jnp.where(q_rows, zero, k * ex).astype(BF16), [slice(0, CHUNK)])
    qs, ks, q_rows = [], [], []
    zero = jnp.zeros((half, HEAD), F32)
    for blk in range(CHUNK // (2 * half)):
        a = blk * 2 * half
        lo, hi = slice(a, a + half), slice(a + half, a + 2 * half)
        ref = jnp.broadcast_to(bc_ref[pl.ds(a + half - 1 if fwd else a + half, 1), :], (half, HEAD))
        if fwd:
            qs.append(q[hi] * jnp.exp(bc[hi] - ref))
            ks += [k[lo] * jnp.exp(ref - bc[lo]), zero]
            q_rows.append(hi)
        else:
            qs.append(q[lo] * jnp.exp(bc[lo] - ref))
            ks += [zero, k[hi] * jnp.exp(ref - bc[hi])]
            q_rows.append(lo)
    return jnp.concatenate(qs, axis=0).astype(BF16), jnp.concatenate(ks, axis=0).astype(BF16), q_rows


def _split3(g):
    hi = g.astype(BF16)
    r1 = g - hi.astype(F32)
    mid = r1.astype(BF16)
    lo = (r1 - mid.astype(F32)).astype(BF16)
    return jnp.concatenate([hi, mid, lo], axis=1)


def _rec_kernel(n_ctx, n_lat, ctx_out, *refs):
    (qx, kfx, kbx, lfx, lbx, vx, zx, qc, kfc, kbc, lfc, lbc, vc, zc,
     tril_ref, triu_ref, lvf_ref, lvb_ref, g_ref) = refs[:19]
    if ctx_out:
        ox_ref, oc_ref = refs[19:21]
        scr = refs[21:]
    else:
        ox_ref, oc_ref = refs[19], None
        scr = refs[20:]
    bcf_scr, bcb_scr, qfb_scr, oin_scr, dst_scr, a_scr, st_scr = scr
    kw = HEAD

    def chunks_a(q_ref, kf_ref, kb_ref, lf_ref, lb_ref, v_ref, chunks, slot0):
        dirs = ((True, kf_ref, lf_ref, tril_ref, lvf_ref, bcf_scr),
                (False, kb_ref, lb_ref, triu_ref, lvb_ref, bcb_scr))
        rows = [pl.ds(pl.multiple_of(c * CHUNK, CHUNK), CHUNK) for c in chunks]
        units = [(u, d) for u in range(len(chunks)) for d in range(2)]
        q = [q_ref[0, r, :].astype(F32) for r in rows]
        v = [v_ref[0, r, :] for r in rows]
        k = {(u, d): dirs[d][1][0, rows[u], :].astype(F32) for u, d in units}
        cs = {(u, d): _dot(dirs[d][3][...], _split3(dirs[d][2][0, rows[u], :])) for u, d in units}
        bc, bref = {}, {}
        for u, d in units:
            bc[u, d] = cs[u, d][:, :kw] + cs[u, d][:, kw:2 * kw] + cs[u, d][:, 2 * kw:]
            bref[u, d] = dirs[d][5].at[u]
            bref[u, d][...] = bc[u, d]
        qi, kd, dec = {}, {}, {}
        for u, d in units:
            b_edge = bref[u, d][pl.ds(CHUNK - 1 if d == 0 else 0, 1), :]
            qi[u, d] = (q[u] * jnp.exp(bc[u, d])).astype(BF16)
            kd[u, d] = (k[u, d] * jnp.exp(b_edge - bc[u, d])).astype(BF16)
            dec[u, d] = jnp.exp(b_edge)
        groups = CHUNK // SUBLANES
        sc = {ud: [jnp.zeros((SUBLANES, CHUNK), F32) for _ in range(groups)] for ud in units}
        for level in range(N_LEVELS):
            for u, d in units:
                qt, kt, q_rows = _level_factors(level, bc[u, d], bref[u, d], q[u], k[u, d], d == 0)
                p = _dot_nt(qt, kt)
                at = 0
                for sl in q_rows:
                    for g in range(sl.start // SUBLANES, sl.stop // SUBLANES):
                        own = dirs[d][4][g * SUBLANES:(g + 1) * SUBLANES, :] == level
                        sc[u, d][g] = jnp.where(own, p[at:at + SUBLANES], sc[u, d][g])
                        at += SUBLANES
        for u in range(len(chunks)):
            slot = slot0 + chunks[u]
            scores = (jnp.concatenate(sc[u, 0], axis=0) + jnp.concatenate(sc[u, 1], axis=0)).astype(BF16)
            oin_scr[slot] = _dot(scores, v[u])
            qfb_scr[slot] = jnp.concatenate([qi[u, 0], qi[u, 1]], axis=1)
            dst_scr[slot] = _dot_tn(v[u], jnp.concatenate([kd[u, 0], kd[u, 1]], axis=1))
            a_scr[slot] = jnp.concatenate([dec[u, 0], dec[u, 1]], axis=1)

    def chunk_c(z_ref, o_ref, c, slot):
        rows = pl.ds(pl.multiple_of(c * CHUNK, CHUNK), CHUNK)
        o = oin_scr[slot] + _dot_nt(qfb_scr[slot], st_scr[slot])
        o = o * lax.rsqrt(jnp.mean(o * o, axis=-1, keepdims=True) + EPS)
        o_ref[0, rows, :] = (o * g_ref[...] * z_ref[0, rows, :].astype(F32)).astype(BF16)

    def for_chunks(n, per_iter, fn):
        per_iter = max(u for u in range(1, per_iter + 1) if n % u == 0)

        def body(i, carry):
            fn([i * per_iter + u for u in range(per_iter)])
            return carry

        lax.fori_loop(0, n // per_iter, body, 0)

    for_chunks(n_ctx, A_UNROLL, lambda cs: chunks_a(qc, kfc, kbc, lfc, lbc, vc, cs, 0))
    for_chunks(n_lat, A_UNROLL, lambda cs: chunks_a(qx, kfx, kbx, lfx, lbx, vx, cs, n_ctx))

    n_all = n_ctx + n_lat
    fwd_order = list(range(n_all))
    bwd_order = list(range(n_ctx - 1, -1, -1)) + list(range(n_all - 1, n_ctx - 1, -1))
    for order, cols in ((fwd_order, slice(0, kw)), (bwd_order, slice(kw, 2 * kw))):
        s = jnp.zeros((kw, kw), F32)
        for slot in order:
            st_scr[slot, :, cols] = s.astype(BF16)
            s = a_scr[slot, :, cols] * s + dst_scr[slot, :, cols]

    for_chunks(n_lat, C_UNROLL, lambda cs: [chunk_c(zx, ox_ref, c, c + n_ctx) for c in cs])
    if ctx_out:
        for_chunks(n_ctx, C_UNROLL, lambda cs: [chunk_c(zc, oc_ref, c, c) for c in cs])


def _recurrence(lat, ctx, rec_g, ctx_out):
    bsz, t, _ = lat[0].shape
    tc = ctx[0].shape[1]
    n_lat, n_ctx = t // CHUNK, tc // CHUNK
    n_all = n_lat + n_ctx
    lvf = _level_ids()
    tri = np.tril(np.ones((CHUNK, CHUNK), np.float32))
    consts = [jnp.asarray(tri, dtype=BF16), jnp.asarray(tri.T, dtype=BF16),
              jnp.asarray(lvf), jnp.asarray(lvf.T)]
    head = lambda n: pl.BlockSpec((1, n, HEAD), lambda b, h: (b, 0, h))
    const = pl.BlockSpec((CHUNK, CHUNK), lambda b, h: (0, 0))
    in_specs = ([head(t)] * 7 + [head(tc)] * 7 + [const] * 4
                + [pl.BlockSpec((1, HEAD), lambda b, h: (0, h))])
    out_specs = [head(t)]
    out_shape = [jax.ShapeDtypeStruct((bsz, t, E_REC), BF16)]
    if ctx_out:
        out_specs.append(head(tc))
        out_shape.append(jax.ShapeDtypeStruct((bsz, tc, E_REC), BF16))
    scratch = [pltpu.VMEM((A_UNROLL, CHUNK, HEAD), F32), pltpu.VMEM((A_UNROLL, CHUNK, HEAD), F32),
               pltpu.VMEM((n_all, CHUNK, 2 * HEAD), BF16),
               pltpu.VMEM((n_all, CHUNK, HEAD), F32),
               pltpu.VMEM((n_all, HEAD, 2 * HEAD), F32),
               pltpu.VMEM((n_all, 1, 2 * HEAD), F32),
               pltpu.VMEM((n_all, HEAD, 2 * HEAD), BF16)]
    out = pl.pallas_call(
        functools.partial(_rec_kernel, n_ctx, n_lat, ctx_out),
        grid=(bsz, REC_HEADS),
        in_specs=in_specs,
        out_specs=out_specs,
        out_shape=out_shape,
        scratch_shapes=scratch,
        compiler_params=_params("parallel", "parallel"),
        name="recurrence",
    )(*lat, *ctx, *consts, rec_g.reshape(1, E_REC))
    return out if ctx_out else (out[0], None)


ROWS = 32


def _mix_kernel(has_pos, last, tm, n_tiles, d_model, *refs):
    refs = list(refs)
    u_ref, up_ref, un_ref, saz_ref, fb_ref, oc_ref, sg_ref, x_ref = refs[:8]
    refs = refs[8:]
    pos_ref = refs.pop(0) if has_pos else None
    gate_ref, cw_ref, cb_ref, lg_ref, lb_ref, wpa_ref, wpb_ref, wpc_ref, wo_ref = refs[:9]
    refs = refs[9:]
    fg_ref = refs.pop(0) if last else None
    o_ref, ush_scr, a_scr = refs

    i = pl.program_id(1)
    prev = jnp.where(i > 0, up_ref[0].astype(F32), 0.0)
    nxt = jnp.where(i < n_tiles - 1, un_ref[0].astype(F32), 0.0)
    ush_scr[0, 0:HALO, :] = prev
    ush_scr[0, HALO:HALO + tm, :] = u_ref[0].astype(F32)
    ush_scr[0, HALO + tm:, :] = nxt
    span = tm + 2 * HALO - SUBLANES
    for r in range(1, SUBLANES):
        ush_scr[r, 0:span, :] = ush_scr[0, r:r + span, :]

    def rows_body(rb, carry):
        base = pl.multiple_of(rb * ROWS, ROWS)
        acc = jnp.broadcast_to(cb_ref[...], (ROWS, E_CONV))
        for j in range(CONV_WIDTH):
            off = j + HALO - CONV_WIDTH // 2
            acc = acc + cw_ref[pl.ds(j, 1), :] * ush_scr[off % SUBLANES,
                                                         pl.ds(base + (off // SUBLANES) * SUBLANES, ROWS), :]
        mu = jnp.mean(acc, axis=-1, keepdims=True)
        xc = acc - mu
        var = jnp.mean(xc * xc, axis=-1, keepdims=True)
        y = xc * lax.rsqrt(var + EPS) * lg_ref[...] + lb_ref[...]
        a_scr[pl.ds(base, ROWS), :] = (_silu(y) * saz_ref[0, pl.ds(base, ROWS), :].astype(F32)).astype(BF16)
        return carry

    lax.fori_loop(0, tm // ROWS, rows_body, 0)

    ya = _dot(a_scr[...], wpa_ref[...])
    yb = _dot(fb_ref[0], wpb_ref[...])
    yc = _dot(oc_ref[0], wpc_ref[...])
    d = d_model
    y = (sg_ref[0, :, 0:d].astype(F32) * ya + sg_ref[0, :, d:2 * d].astype(F32) * yb
         + sg_ref[0, :, 2 * d:3 * d].astype(F32) * yc)
    z = _dot(y.astype(BF16), wo_ref[...])
    x = x_ref[0]
    if has_pos:
        x = x + pos_ref[...]
    xn = x + gate_ref[0] * z
    if last:
        xn = xn * lax.rsqrt(jnp.mean(xn * xn, axis=-1, keepdims=True) + EPS) * fg_ref[...]
    o_ref[0] = xn


def _mix(u, saz, fb, oc, sg, x, pos, gate, cw, cb, lg, lb, wpa, wpb, wpc, wo, final_g, tm):
    bsz, t, d = x.shape
    n_tiles = t // tm
    has_pos = pos is not None
    last = final_g is not None
    hb = tm // HALO
    n_hb = t // HALO
    tok = lambda w: pl.BlockSpec((1, tm, w), lambda b, i: (b, i, 0))
    full = lambda a: pl.BlockSpec(a.shape, lambda b, i: (0,) * a.ndim)
    in_specs = [tok(E_CONV),
                pl.BlockSpec((1, HALO, E_CONV), lambda b, i: (b, jnp.maximum(i * hb - 1, 0), 0)),
                pl.BlockSpec((1, HALO, E_CONV), lambda b, i: (b, jnp.minimum((i + 1) * hb, n_hb - 1), 0)),
                tok(E_CONV), tok(E_FOURIER), tok(E_REC), tok(N_BRANCH * d), tok(d)]
    args = [u, u, u, saz, fb, oc, sg, x]
    if has_pos:
        in_specs.append(pl.BlockSpec((tm, d), lambda b, i: (i, 0)))
        args.append(pos)
    params = [cw, cb, lg, lb, wpa, wpb, wpc, wo]
    in_specs += [pl.BlockSpec((1, 1, d), lambda b, i: (b, 0, 0))] + [full(a) for a in params]
    args += [gate] + params
    if last:
        in_specs.append(full(final_g))
        args.append(final_g)
    return pl.pallas_call(
        functools.partial(_mix_kernel, has_pos, last, tm, n_tiles, d),
        grid=(bsz, n_tiles),
        in_specs=in_specs,
        out_specs=tok(d),
        out_shape=jax.ShapeDtypeStruct((bsz, t, d), F32),
        scratch_shapes=[pltpu.VMEM((SUBLANES, tm + 2 * HALO, E_CONV), F32),
                        pltpu.VMEM((tm, E_CONV), BF16)],
        compiler_params=_params("parallel", "parallel"),
        name="mix",
    )(*args)


def _sincos_2d(rows, cols, d):
    quarter = d // 4
    omega = 1.0 / (10000.0 ** (jnp.arange(quarter, dtype=F32) / quarter))
    er = jnp.arange(rows, dtype=F32)[:, None] * omega
    ec = jnp.arange(cols, dtype=F32)[:, None] * omega
    emb_r = jnp.concatenate([jnp.sin(er), jnp.cos(er)], axis=-1)
    emb_c = jnp.concatenate([jnp.sin(ec), jnp.cos(ec)], axis=-1)
    emb = jnp.concatenate([jnp.broadcast_to(emb_r[:, None, :], (rows, cols, d // 2)),
                           jnp.broadcast_to(emb_c[None, :, :], (rows, cols, d // 2))], axis=-1)
    return emb.reshape(rows * cols, d)


def _tile(n, want):
    return want if n % want == 0 else n


def kernel(x, c, ctx, c_ctx, w_ada, b_ada, norm_g, w_in, conv_w, conv_b, conv_ln_g, conv_ln_b,
           rec_lb, rec_norm_g, w_pa, w_pb, w_pc, w_out, final_g):
    bsz, t, d = x.shape
    tc = ctx.shape[1]
    depth = w_in.shape[0]
    assert t % CHUNK == 0 and tc % CHUNK == 0 and t % GRID_W == 0

    pos = _sincos_2d(t // GRID_W, GRID_W, d)
    lb_soft = jax.nn.softmax(rec_lb.astype(F32), axis=1)
    lbs = jnp.cumsum(lb_soft, axis=1) - lb_soft[:, :1]

    rows = -(-(bsz + 1) // SUBLANES) * SUBLANES
    cc = jnp.zeros((rows, d), F32).at[:bsz].set(c).at[bsz].set(c_ctx)
    mod = _ada(cc, w_ada, b_ada)

    tm_x, tm_c = _tile(t, 256), _tile(tc, 256)
    for l in range(depth):
        last = l == depth - 1
        shift, scale, gate = mod[l, :, :d], mod[l, :, d:2 * d], mod[l, :, 2 * d:]
        mult = norm_g[l][None, :] * (1.0 + scale)
        mult_x, shift_x, gate_x = (a[:bsz, None, :] for a in (mult, shift, gate))
        mult_c, shift_c, gate_c = (jnp.broadcast_to(a[bsz][None, None, :], (bsz, 1, d))
                                   for a in (mult, shift, gate))
        w_l = w_in[l].astype(BF16)
        omlb = 1.0 - lbs[:, l, :]

        px = _inproj(x, pos if l == 0 else None, mult_x, shift_x, w_l, omlb, tm_x)
        pc = _inproj(ctx, None, mult_c, shift_c, w_l, omlb, tm_c)

        ox, oc = _recurrence(px[4:11], pc[4:11], rec_norm_g[l], ctx_out=not last)

        mixw = (conv_w[l].reshape(CONV_WIDTH, E_CONV), conv_b[l][None], conv_ln_g[l][None],
                conv_ln_b[l][None], w_pa[l].astype(BF16), w_pb[l].astype(BF16),
                w_pc[l].astype(BF16), w_out[l].astype(BF16))
        fbx = _fourier(px[2], px[3], _tile(t, 512))
        x = _mix(px[0], px[1], fbx, ox, px[11], x, pos if l == 0 else None, gate_x, *mixw,
                 final_g[None] if last else None, tm_x)
        if not last:
            fbc = _fourier(pc[2], pc[3], _tile(tc, 512))
            ctx = _mix(pc[0], pc[1], fbc, oc, pc[11], ctx, None, gate_c, *mixw, None, tm_c)
    return x
```

```python
import functools

import numpy as np
import jax
import jax.numpy as jnp
from jax import lax
from jax.experimental import pallas as pl
from jax.experimental.pallas import tpu as pltpu

F32 = jnp.float32
BF16 = jnp.bfloat16

E_CONV = 768
CONV_WIDTH = 31
E_FOURIER = 512
FOURIER_GROUPS = 4
E_REC = 768
REC_HEADS = 6
HEAD = E_REC // REC_HEADS
N_BRANCH = 3
EPS = 1e-6
K_MAX = 1.0 - 1e-6
GRID_W = 64

CHUNK = 128
BASE = 4
SUBLANES = 8
HALO = 16
CONV_ROWS = 32
VMEM_LIMIT = 56 * 1024 * 1024


def _sigmoid(x):
    return 0.5 * jnp.tanh(0.5 * x) + 0.5


def _silu(x):
    return x * _sigmoid(x)


def _dot(a, b):
    return jnp.dot(a, b, preferred_element_type=F32)


def _dot_nt(a, b):
    return lax.dot_general(a, b, (((1,), (1,)), ((), ())), preferred_element_type=F32)


def _dot_tn(a, b):
    return lax.dot_general(a, b, (((0,), (0,)), ((), ())), preferred_element_type=F32)


def _params(*sem):
    return pltpu.CompilerParams(dimension_semantics=sem, vmem_limit_bytes=VMEM_LIMIT)


def _ada_kernel(cc_ref, w_ref, b_ref, o_ref):
    s = _silu(cc_ref[...])
    o_ref[0] = jnp.dot(s, w_ref[0], preferred_element_type=F32,
                       precision=lax.Precision.HIGHEST) + b_ref[0]


def _ada(cc, w_ada, b_ada):
    depth, d, d3 = w_ada.shape
    rows = cc.shape[0]
    return pl.pallas_call(
        _ada_kernel,
        grid=(depth, d3 // d),
        in_specs=[pl.BlockSpec((rows, d), lambda l, j: (0, 0)),
                  pl.BlockSpec((1, d, d), lambda l, j: (l, 0, j)),
                  pl.BlockSpec((1, 1, d), lambda l, j: (l, 0, j))],
        out_specs=pl.BlockSpec((1, rows, d), lambda l, j: (l, 0, j)),
        out_shape=jax.ShapeDtypeStruct((depth, rows, d3), F32),
        compiler_params=_params("arbitrary", "arbitrary"),
        name="ada",
    )(cc, w_ada, b_ada.reshape(depth, 1, d3))


_SEC = {}
_off = 0
for _name, _size in (("a_val", E_CONV), ("a_gate", E_CONV), ("a_z", E_CONV),
                     ("b_u", E_FOURIER), ("b_z", E_FOURIER),
                     ("q", E_REC), ("f_fwd", E_REC), ("f_bwd", E_REC), ("i", E_REC), ("c_z", E_REC)):
    _SEC[_name] = (_off, _off + _size)
    _off += _size
GATES_OFF = _off


def _inproj_kernel(has_pos, d_model, tm, n_tiles, *refs):
    refs = list(refs)
    x_ref = refs.pop(0)
    pos_ref = refs.pop(0) if has_pos else None
    (mult_ref, shift_ref, w_ref, omlb_ref, cw_ref, cb_ref, lg_ref, lb_ref,
     a_ref, bu_ref, sbz_ref, q_ref, kf_ref, kb_ref, lf_ref, lbw_ref, v_ref, scz_ref, sg_ref,
     win_scr, ush_scr, saz_scr) = refs

    g = pl.program_id(0)

    @pl.when(g == 0)
    def _():
        win_scr[...] = jnp.zeros_like(win_scr)
        saz_scr[...] = jnp.zeros_like(saz_scr)

    x = x_ref[0]
    if has_pos:
        x = x + pos_ref[...]
    ms = jnp.mean(x * x, axis=-1, keepdims=True)
    h = (x * lax.rsqrt(ms + EPS) * mult_ref[0] + shift_ref[0]).astype(BF16)

    def proj(first, last):
        lo, hi = _SEC[first][0], _SEC[last][1]
        r = _dot(h, w_ref[:, lo:hi])
        return lambda name: r[:, _SEC[name][0] - lo:_SEC[name][1] - lo]

    starts = (g % n_tiles) == 0
    conv_p = proj("a_val", "a_z")
    u_new = conv_p("a_val") * _sigmoid(conv_p("a_gate"))
    saz_scr[g % 2] = _silu(conv_p("a_z")).astype(BF16)
    win_scr[HALO + tm:, :] = jnp.where(starts, 0.0, u_new[0:HALO])
    span = tm + 2 * HALO - SUBLANES
    for r in range(SUBLANES):
        ush_scr[r, 0:span, :] = win_scr[r:r + span, :]
    win_scr[0:HALO, :] = jnp.where(starts, 0.0, win_scr[tm:tm + HALO, :])
    win_scr[HALO:HALO + tm, :] = u_new

    def conv_rows(blk):
        base = blk * CONV_ROWS
        acc = jnp.broadcast_to(cb_ref[...], (CONV_ROWS, E_CONV))
        for j in range(CONV_WIDTH):
            off = j + HALO - CONV_WIDTH // 2
            lo = base + (off // SUBLANES) * SUBLANES
            tap = jnp.concatenate([cw_ref[j]] * (CONV_ROWS // SUBLANES), axis=0)
            acc = acc + tap * ush_scr[off % SUBLANES, lo:lo + CONV_ROWS, :]
        mu = jnp.mean(acc, axis=-1, keepdims=True)
        xc = acc - mu
        var = jnp.mean(xc * xc, axis=-1, keepdims=True)
        y = xc * lax.rsqrt(var + EPS) * lg_ref[...] + lb_ref[...]
        gate_prev = saz_scr[(g + 1) % 2, base:base + CONV_ROWS, :]
        a_ref[0, base:base + CONV_ROWS, :] = (_silu(y) * gate_prev.astype(F32)).astype(BF16)

    def sec_fourier():
        p = proj("b_u", "b_z")
        bu_ref[0] = p("b_u").astype(BF16)
        sbz_ref[0] = _silu(p("b_z")).astype(BF16)

    def sec_rec():
        p = proj("q", "c_z")
        q_ref[0] = p("q").astype(BF16)
        for name, k_ref, l_ref, row in (("f_fwd", kf_ref, lf_ref, 0), ("f_bwd", kb_ref, lbw_ref, 1)):
            k = jnp.minimum(omlb_ref[row:row + 1, :] * _sigmoid(-p(name)), K_MAX)
            k_ref[0] = k.astype(BF16)
            l_ref[0] = jnp.log(1.0 - k)
        v_ref[0] = p("i").astype(BF16)
        scz_ref[0] = _silu(p("c_z")).astype(BF16)

    def sec_gates():
        sg_ref[0] = _dot(h, w_ref[:, GATES_OFF:GATES_OFF + N_BRANCH * d_model]).astype(BF16)

    n_blk = tm // CONV_ROWS
    cuts = (n_blk // 4, (3 * n_blk) // 4, n_blk)
    for sec, lo, hi in ((sec_fourier, 0, cuts[0]), (sec_rec, cuts[0], cuts[1]), (sec_gates, cuts[1], cuts[2])):
        sec()
        for blk in range(lo, hi):
            conv_rows(blk)


def _inproj(x, pos, mult, shift, w_bf16, omlb, conv, tm):
    bsz, t, d = x.shape
    d_in = w_bf16.shape[1]
    has_pos = pos is not None
    n_tiles = t // tm
    n_all = bsz * n_tiles
    cur = lambda g: jnp.minimum(g, n_all - 1)
    prev = lambda g: jnp.maximum(g - 1, 0)
    tok = lambda w, at=cur: pl.BlockSpec((1, tm, w), lambda g: (at(g) // n_tiles, at(g) % n_tiles, 0))
    full = lambda arr: pl.BlockSpec(arr.shape, lambda g: (0,) * arr.ndim)
    in_specs = [tok(d)]
    args = [x]
    if has_pos:
        in_specs.append(pl.BlockSpec((tm, d), lambda g: (cur(g) % n_tiles, 0)))
        args.append(pos)
    in_specs += [pl.BlockSpec((1, 1, d), lambda g: (cur(g) // n_tiles, 0, 0)),
                 pl.BlockSpec((1, 1, d), lambda g: (cur(g) // n_tiles, 0, 0)),
                 pl.BlockSpec((d, d_in), lambda g: (0, 0), pipeline_mode=pl.Buffered(1)),
                 full(omlb)] + [full(c) for c in conv]
    args += [mult, shift, w_bf16, omlb, *conv]
    widths = [(E_FOURIER, BF16), (E_FOURIER, BF16),
              (E_REC, BF16), (E_REC, BF16), (E_REC, BF16), (E_REC, F32), (E_REC, F32),
              (E_REC, BF16), (E_REC, BF16), (N_BRANCH * d, BF16)]
    return pl.pallas_call(
        functools.partial(_inproj_kernel, has_pos, d, tm, n_tiles),
        grid=(n_all + 1,),
        in_specs=in_specs,
        out_specs=[tok(E_CONV, prev)] + [tok(w) for w, _ in widths],
        out_shape=[jax.ShapeDtypeStruct((bsz, t, w), dt) for w, dt in [(E_CONV, BF16)] + widths],
        scratch_shapes=[pltpu.VMEM((tm + 2 * HALO, E_CONV), F32),
                        pltpu.VMEM((SUBLANES, tm + 2 * HALO, E_CONV), F32),
                        pltpu.VMEM((2, tm, E_CONV), BF16)],
        compiler_params=_params("arbitrary"),
        name="inproj",
    )(*args)


def _fourier_kernel(t, scale, u_ref, gate_ref, cs_ref, cc_ref, o_ref, ab_scr):
    gw = E_FOURIER // FOURIER_GROUPS

    @pl.when(pl.program_id(1) == 0)
    def _():
        for g in range(FOURIER_GROUPS):
            r = _dot(u_ref[0, :, g * gw:(g + 1) * gw], cc_ref[...])
            ab_scr[0:t, g * gw:(g + 1) * gw] = r[:, :gw].astype(BF16)
            ab_scr[t:2 * t, g * gw:(g + 1) * gw] = r[:, gw:].astype(BF16)

    f = _dot(cs_ref[...], ab_scr[...])
    o_ref[0] = (f * scale * gate_ref[0].astype(F32)).astype(BF16)


def _dft_consts(t):
    gw = E_FOURIER // FOURIER_GROUPS
    jk = np.outer(np.arange(t), np.arange(t)) % t
    ang = 2.0 * np.pi * jk / t
    cs = np.concatenate([np.cos(ang), -np.sin(ang)], axis=1)
    jc = np.outer(np.arange(gw), np.arange(gw)) % gw
    angc = 2.0 * np.pi * jc / gw
    cc = np.concatenate([np.cos(angc), np.sin(angc)], axis=1)
    scale = 1.0 / np.sqrt(float(t) * gw)
    return jnp.asarray(cs, dtype=BF16), jnp.asarray(cc, dtype=BF16), float(scale)


def _fourier(bu, sbz, tr):
    bsz, t, e = bu.shape
    gw = e // FOURIER_GROUPS
    cs, cc, scale = _dft_consts(t)
    return pl.pallas_call(
        functools.partial(_fourier_kernel, t, scale),
        grid=(bsz, t // tr),
        in_specs=[pl.BlockSpec((1, t, e), lambda b, i: (b, 0, 0)),
                  pl.BlockSpec((1, tr, e), lambda b, i: (b, i, 0)),
                  pl.BlockSpec((tr, 2 * t), lambda b, i: (i, 0)),
                  pl.BlockSpec((gw, 2 * gw), lambda b, i: (0, 0))],
        out_specs=pl.BlockSpec((1, tr, e), lambda b, i: (b, i, 0)),
        out_shape=jax.ShapeDtypeStruct((bsz, t, e), BF16),
        scratch_shapes=[pltpu.VMEM((2 * t, e), BF16)],
        compiler_params=_params("parallel", "arbitrary"),
        name="fourier",
    )(bu, sbz, cs, cc)


def _level_ids():
    t = np.arange(CHUNK)[:, None]
    s = np.arange(CHUNK)[None, :]
    lv = np.full((CHUNK, CHUNK), -1, np.int32)
    size, level = CHUNK, int(np.log2(CHUNK // BASE))
    while size >= BASE:
        lv = np.where((t // size == s // size) & (s <= t), level, lv)
        size //= 2
        level -= 1
    return lv.astype(np.int32)


N_LEVELS = int(np.log2(CHUNK // BASE)) + 1
A_UNROLL = 8
C_UNROLL = 4


def _row(ref, r):
    return jnp.broadcast_to(ref[pl.ds(r, 1), :], (SUBLANES, HEAD))


def _level_factors(level, bc, bc_ref, q, k, fwd):
    groups = CHUNK // SUBLANES
    sub = lax.broadcasted_iota(jnp.int32, (CHUNK, HEAD), 0) % SUBLANES
    if level == 0:
        lo_row, hi_row = (0, BASE) if fwd else (BASE - 1, SUBLANES - 1)
        ref = jnp.concatenate(
            [jnp.where(sub[:SUBLANES] < BASE, _row(bc_ref, g * SUBLANES + lo_row),
                       _row(bc_ref, g * SUBLANES + hi_row)) for g in range(groups)], axis=0)
        e = bc - ref
        return (q * jnp.exp(e)).astype(BF16), (k * jnp.exp(-e)).astype(BF16), [slice(0, CHUNK)]
    half = BASE << (level - 1)
    if half < SUBLANES:
        ref_row = half - 1 if fwd else half
        ref = jnp.concatenate([_row(bc_ref, g * SUBLANES + ref_row) for g in range(groups)], axis=0)
        q_rows = (sub >= half) if fwd else (sub < half)
        ex = jnp.exp(jnp.where(q_rows, bc - ref, ref - bc))
        zero = jnp.zeros_like(ex)
        return (jnp.where(q_rows, q * ex, zero).astype(BF16),
                jnp.where(q_rows, zero, k * ex).astype(BF16), [slice(0, CHUNK)])
    qs, ks, q_rows = [], [], []
    zero = jnp.zeros((half, HEAD), F32)
    for blk in range(CHUNK // (2 * half)):
        a = blk * 2 * half
        lo, hi = slice(a, a + half), slice(a + half, a + 2 * half)
        ref = jnp.broadcast_to(bc_ref[pl.ds(a + half - 1 if fwd else a + half, 1), :], (half, HEAD))
        if fwd:
            qs.append(q[hi] * jnp.exp(bc[hi] - ref))
            ks += [k[lo] * jnp.exp(ref - bc[lo]), zero]
            q_rows.append(hi)
        else:
            qs.append(q[lo] * jnp.exp(bc[lo] - ref))
            ks += [zero, k[hi] * jnp.exp(ref - bc[hi])]
            q_rows.append(lo)
    return jnp.concatenate(qs, axis=0).astype(BF16), jnp.concatenate(ks, axis=0).astype(BF16), q_rows


def _split3(g):
    hi = g.astype(BF16)
    r1 = g - hi.astype(F32)
    mid = r1.astype(BF16)
    lo = (r1 - mid.astype(F32)).astype(BF16)
    return jnp.concatenate([hi, mid, lo], axis=1)


def _rec_kernel(n_ctx, n_lat, ctx_out, *refs):
    (qx, kfx, kbx, lfx, lbx, vx, zx, qc, kfc, kbc, lfc, lbc, vc, zc,
     tril_ref, triu_ref, lvf_ref, lvb_ref, g_ref) = refs[:19]
    if ctx_out:
        ox_ref, oc_ref = refs[19:21]
        scr = refs[21:]
    else:
        ox_ref, oc_ref = refs[19], None
        scr = refs[20:]
    bcf_scr, bcb_scr, qfb_scr, oin_scr, dst_scr, a_scr, st_scr = scr
    kw = HEAD

    def chunks_a(q_ref, kf_ref, kb_ref, lf_ref, lb_ref, v_ref, chunks, slot0):
        dirs = ((True, kf_ref, lf_ref, tril_ref, lvf_ref, bcf_scr),
                (False, kb_ref, lb_ref, triu_ref, lvb_ref, bcb_scr))
        rows = [pl.ds(pl.multiple_of(c * CHUNK, CHUNK), CHUNK) for c in chunks]
        units = [(u, d) for u in range(len(chunks)) for d in range(2)]
        q = [q_ref[0, r, :].astype(F32) for r in rows]
        v = [v_ref[0, r, :] for r in rows]
        k = {(u, d): dirs[d][1][0, rows[u], :].astype(F32) for u, d in units}
        cs = {(u, d): _dot(dirs[d][3][...], _split3(dirs[d][2][0, rows[u], :])) for u, d in units}
        bc, bref = {}, {}
        for u, d in units:
            bc[u, d] = cs[u, d][:, :kw] + cs[u, d][:, kw:2 * kw] + cs[u, d][:, 2 * kw:]
            bref[u, d] = dirs[d][5].at[u]
            bref[u, d][...] = bc[u, d]
        qi, kd, dec = {}, {}, {}
        for u, d in units:
            b_edge = bref[u, d][pl.ds(CHUNK - 1 if d == 0 else 0, 1), :]
            qi[u, d] = (q[u] * jnp.exp(bc[u, d])).astype(BF16)
            kd[u, d] = (k[u, d] * jnp.exp(b_edge - bc[u, d])).astype(BF16)
            dec[u, d] = jnp.exp(b_edge)
        groups = CHUNK // SUBLANES
        sc = {ud: [jnp.zeros((SUBLANES, CHUNK), F32) for _ in range(groups)] for ud in units}
        for level in range(N_LEVELS):
            for u, d in units:
                qt, kt, q_rows = _level_factors(level, bc[u, d], bref[u, d], q[u], k[u, d], d == 0)
                p = _dot_nt(qt, kt)
                at = 0
                for sl in q_rows:
                    for g in range(sl.start // SUBLANES, sl.stop // SUBLANES):
                        own = dirs[d][4][g * SUBLANES:(g + 1) * SUBLANES, :] == level
                        sc[u, d][g] = jnp.where(own, p[at:at + SUBLANES], sc[u, d][g])
                        at += SUBLANES
        for u in range(len(chunks)):
            slot = slot0 + chunks[u]
            scores = (jnp.concatenate(sc[u, 0], axis=0) + jnp.concatenate(sc[u, 1], axis=0)).astype(BF16)
            oin_scr[slot] = _dot(scores, v[u])
            qfb_scr[slot] = jnp.concatenate([qi[u, 0], qi[u, 1]], axis=1)
            dst_scr[slot] = _dot_tn(v[u], jnp.concatenate([kd[u, 0], kd[u, 1]], axis=1))
            a_scr[slot] = jnp.concatenate([dec[u, 0], dec[u, 1]], axis=1)

    def chunk_c(z_ref, o_ref, c, slot):
        rows = pl.ds(pl.multiple_of(c * CHUNK, CHUNK), CHUNK)
        o = oin_scr[slot] + _dot_nt(qfb_scr[slot], st_scr[slot])
        o = o * lax.rsqrt(jnp.mean(o * o, axis=-1, keepdims=True) + EPS)
        o_ref[0, rows, :] = (o * g_ref[...] * z_ref[0, rows, :].astype(F32)).astype(BF16)

    def for_chunks(n, per_iter, fn):
        per_iter = max(u for u in range(1, per_iter + 1) if n % u == 0)

        def body(i, carry):
            fn([i * per_iter + u for u in range(per_iter)])
            return carry

        lax.fori_loop(0, n // per_iter, body, 0)

    for_chunks(n_ctx, A_UNROLL, lambda cs: chunks_a(qc, kfc, kbc, lfc, lbc, vc, cs, 0))
    for_chunks(n_lat, A_UNROLL, lambda cs: chunks_a(qx, kfx, kbx, lfx, lbx, vx, cs, n_ctx))

    n_all = n_ctx + n_lat
    fwd_order = list(range(n_all))
    bwd_order = list(range(n_ctx - 1, -1, -1)) + list(range(n_all - 1, n_ctx - 1, -1))
    for order, cols in ((fwd_order, slice(0, kw)), (bwd_order, slice(kw, 2 * kw))):
        s = jnp.zeros((kw, kw), F32)
        for slot in order:
            st_scr[slot, :, cols] = s.astype(BF16)
            s = a_scr[slot, :, cols] * s + dst_scr[slot, :, cols]

    for_chunks(n_lat, C_UNROLL, lambda cs: [chunk_c(zx, ox_ref, c, c + n_ctx) for c in cs])
    if ctx_out:
        for_chunks(n_ctx, C_UNROLL, lambda cs: [chunk_c(zc, oc_ref, c, c) for c in cs])


def _recurrence(lat, ctx, rec_g, ctx_out):
    bsz, t, _ = lat[0].shape
    tc = ctx[0].shape[1]
    n_lat, n_ctx = t // CHUNK, tc // CHUNK
    n_all = n_lat + n_ctx
    lvf = _level_ids()
    tri = np.tril(np.ones((CHUNK, CHUNK), np.float32))
    consts = [jnp.asarray(tri, dtype=BF16), jnp.asarray(tri.T, dtype=BF16),
              jnp.asarray(lvf), jnp.asarray(lvf.T)]
    head = lambda n: pl.BlockSpec((1, n, HEAD), lambda b, h: (b, 0, h))
    const = pl.BlockSpec((CHUNK, CHUNK), lambda b, h: (0, 0))
    in_specs = ([head(t)] * 7 + [head(tc)] * 7 + [const] * 4
                + [pl.BlockSpec((1, HEAD), lambda b, h: (0, h))])
    out_specs = [head(t)]
    out_shape = [jax.ShapeDtypeStruct((bsz, t, E_REC), BF16)]
    if ctx_out:
        out_specs.append(head(tc))
        out_shape.append(jax.ShapeDtypeStruct((bsz, tc, E_REC), BF16))
    scratch = [pltpu.VMEM((A_UNROLL, CHUNK, HEAD), F32), pltpu.VMEM((A_UNROLL, CHUNK, HEAD), F32),
               pltpu.VMEM((n_all, CHUNK, 2 * HEAD), BF16),
               pltpu.VMEM((n_all, CHUNK, HEAD), F32),
               pltpu.VMEM((n_all, HEAD, 2 * HEAD), F32),
               pltpu.VMEM((n_all, 1, 2 * HEAD), F32),
               pltpu.VMEM((n_all, HEAD, 2 * HEAD), BF16)]
    out = pl.pallas_call(
        functools.partial(_rec_kernel, n_ctx, n_lat, ctx_out),
        grid=(bsz, REC_HEADS),
        in_specs=in_specs,
        out_specs=out_specs,
        out_shape=out_shape,
        scratch_shapes=scratch,
        compiler_params=_params("parallel", "parallel"),
        name="recurrence",
    )(*lat, *ctx, *consts, rec_g.reshape(1, E_REC))
    return out if ctx_out else (out[0], None)


def _mix_kernel(has_pos, last, d_model, *refs):
    refs = list(refs)
    a_ref, fb_ref, oc_ref, sg_ref, x_ref = refs[:5]
    refs = refs[5:]
    pos_ref = refs.pop(0) if has_pos else None
    gate_ref, wpa_ref, wpb_ref, wpc_ref, wo_ref = refs[:5]
    refs = refs[5:]
    fg_ref = refs.pop(0) if last else None
    (o_ref,) = refs

    ya = _dot(a_ref[0], wpa_ref[...])
    yb = _dot(fb_ref[0], wpb_ref[...])
    yc = _dot(oc_ref[0], wpc_ref[...])
    d = d_model
    merge = lambda j: _sigmoid(sg_ref[0, :, j * d:(j + 1) * d].astype(F32))
    y = merge(0) * ya + merge(1) * yb + merge(2) * yc
    z = _dot(y.astype(BF16), wo_ref[...])
    x = x_ref[0]
    if has_pos:
        x = x + pos_ref[...]
    xn = x + gate_ref[0] * z
    if last:
        xn = xn * lax.rsqrt(jnp.mean(xn * xn, axis=-1, keepdims=True) + EPS) * fg_ref[...]
    o_ref[0] = xn


def _mix(a, fb, oc, sg, x, pos, gate, wpa, wpb, wpc, wo, final_g, tm):
    bsz, t, d = x.shape
    has_pos = pos is not None
    last = final_g is not None
    tok = lambda w: pl.BlockSpec((1, tm, w), lambda b, i: (b, i, 0))
    full = lambda arr: pl.BlockSpec(arr.shape, lambda b, i: (0,) * arr.ndim)
    in_specs = [tok(E_CONV), tok(E_FOURIER), tok(E_REC), tok(N_BRANCH * d), tok(d)]
    args = [a, fb, oc, sg, x]
    if has_pos:
        in_specs.append(pl.BlockSpec((tm, d), lambda b, i: (i, 0)))
        args.append(pos)
    params = [wpa, wpb, wpc, wo]
    in_specs += [pl.BlockSpec((1, 1, d), lambda b, i: (b, 0, 0))] + [full(p) for p in params]
    args += [gate] + params
    if last:
        in_specs.append(full(final_g))
        args.append(final_g)
    return pl.pallas_call(
        functools.partial(_mix_kernel, has_pos, last, d),
        grid=(bsz, t // tm),
        in_specs=in_specs,
        out_specs=tok(d),
        out_shape=jax.ShapeDtypeStruct((bsz, t, d), F32),
        compiler_params=_params("parallel", "parallel"),
        name="mix",
    )(*args)


def _sincos_2d(rows, cols, d):
    quarter = d // 4
    omega = 1.0 / (10000.0 ** (jnp.arange(quarter, dtype=F32) / quarter))
    er = jnp.arange(rows, dtype=F32)[:, None] * omega
    ec = jnp.arange(cols, dtype=F32)[:, None] * omega
    emb_r = jnp.concatenate([jnp.sin(er), jnp.cos(er)], axis=-1)
    emb_c = jnp.concatenate([jnp.sin(ec), jnp.cos(ec)], axis=-1)
    emb = jnp.concatenate([jnp.broadcast_to(emb_r[:, None, :], (rows, cols, d // 2)),
                           jnp.broadcast_to(emb_c[None, :, :], (rows, cols, d // 2))], axis=-1)
    return emb.reshape(rows * cols, d)


def _tile(n, want):
    return want if n % want == 0 else n


def kernel(x, c, ctx, c_ctx, w_ada, b_ada, norm_g, w_in, conv_w, conv_b, conv_ln_g, conv_ln_b,
           rec_lb, rec_norm_g, w_pa, w_pb, w_pc, w_out, final_g):
    bsz, t, d = x.shape
    tc = ctx.shape[1]
    depth = w_in.shape[0]
    assert t % CHUNK == 0 and tc % CHUNK == 0 and t % GRID_W == 0

    pos = _sincos_2d(t // GRID_W, GRID_W, d)
    lb_soft = jax.nn.softmax(rec_lb.astype(F32), axis=1)
    lbs = jnp.cumsum(lb_soft, axis=1) - lb_soft[:, :1]

    rows = -(-(bsz + 1) // SUBLANES) * SUBLANES
    cc = jnp.zeros((rows, d), F32).at[:bsz].set(c).at[bsz].set(c_ctx)
    mod = _ada(cc, w_ada, b_ada)

    tm_x, tm_c = _tile(t, 256), _tile(tc, 256)
    for l in range(depth):
        last = l == depth - 1
        shift, scale, gate = mod[l, :, :d], mod[l, :, d:2 * d], mod[l, :, 2 * d:]
        mult = norm_g[l][None, :] * (1.0 + scale)
        mult_x, shift_x, gate_x = (a[:bsz, None, :] for a in (mult, shift, gate))
        mult_c, shift_c, gate_c = (jnp.broadcast_to(a[bsz][None, None, :], (bsz, 1, d))
                                   for a in (mult, shift, gate))
        w_l = w_in[l].astype(BF16)
        omlb = 1.0 - lbs[:, l, :]
        conv = (jnp.broadcast_to(conv_w[l].reshape(CONV_WIDTH, 1, E_CONV), (CONV_WIDTH, SUBLANES, E_CONV)),
                conv_b[l][None], conv_ln_g[l][None], conv_ln_b[l][None])

        px = _inproj(x, pos if l == 0 else None, mult_x, shift_x, w_l, omlb, conv, tm_x)
        pc = _inproj(ctx, None, mult_c, shift_c, w_l, omlb, conv, tm_c)

        ox, oc = _recurrence(px[3:10], pc[3:10], rec_norm_g[l], ctx_out=not last)

        mixw = (w_pa[l].astype(BF16), w_pb[l].astype(BF16), w_pc[l].astype(BF16), w_out[l].astype(BF16))
        fbx = _fourier(px[1], px[2], _tile(t, 512))
        x = _mix(px[0], fbx, ox, px[10], x, pos if l == 0 else None, gate_x, *mixw,
                 final_g[None] if last else None, _tile(t, 512))
        if not last:
            fbc = _fourier(pc[1], pc[2], _tile(tc, 512))
            ctx = _mix(pc[0], fbc, oc, pc[10], ctx, None, gate_c, *mixw, None, tm_c)
    return x
```

```python
import functools

import numpy as np
import jax
import jax.numpy as jnp
from jax import lax
from jax.experimental import pallas as pl
from jax.experimental.pallas import tpu as pltpu

F32 = jnp.float32
BF16 = jnp.bfloat16

E_CONV = 768
CONV_WIDTH = 31
E_FOURIER = 512
FOURIER_GROUPS = 4
E_REC = 768
REC_HEADS = 6
HEAD = E_REC // REC_HEADS
N_BRANCH = 3
EPS = 1e-6
K_MAX = 1.0 - 1e-6
LOG2_E = 1.4426950408889634
GRID_W = 64

CHUNK = 128
BASE = 4
SUBLANES = 8
HALO = 16
CONV_ROWS = 32
VMEM_LIMIT = 56 * 1024 * 1024


def _sigmoid(x):
    return 0.5 * jnp.tanh(0.5 * x) + 0.5


def _silu(x):
    return x * _sigmoid(x)


def _dot(a, b):
    return jnp.dot(a, b, preferred_element_type=F32)


def _dot_nt(a, b):
    return lax.dot_general(a, b, (((1,), (1,)), ((), ())), preferred_element_type=F32)


def _dot_tn(a, b):
    return lax.dot_general(a, b, (((0,), (0,)), ((), ())), preferred_element_type=F32)


def _params(*sem):
    return pltpu.CompilerParams(dimension_semantics=sem, vmem_limit_bytes=VMEM_LIMIT)


def _ada_kernel(cc_ref, w_ref, b_ref, o_ref):
    s = _silu(cc_ref[...])
    o_ref[0] = jnp.dot(s, w_ref[0], preferred_element_type=F32,
                       precision=lax.Precision.HIGHEST) + b_ref[0]


def _ada(cc, w_ada, b_ada):
    depth, d, d3 = w_ada.shape
    rows = cc.shape[0]
    return pl.pallas_call(
        _ada_kernel,
        grid=(depth, d3 // d),
        in_specs=[pl.BlockSpec((rows, d), lambda l, j: (0, 0)),
                  pl.BlockSpec((1, d, d), lambda l, j: (l, 0, j)),
                  pl.BlockSpec((1, 1, d), lambda l, j: (l, 0, j))],
        out_specs=pl.BlockSpec((1, rows, d), lambda l, j: (l, 0, j)),
        out_shape=jax.ShapeDtypeStruct((depth, rows, d3), F32),
        compiler_params=_params("arbitrary", "arbitrary"),
        name="ada",
    )(cc, w_ada, b_ada.reshape(depth, 1, d3))


_SEC = {}
_off = 0
for _name, _size in (("a_val", E_CONV), ("a_gate", E_CONV), ("a_z", E_CONV),
                     ("b_u", E_FOURIER), ("b_z", E_FOURIER),
                     ("q", E_REC), ("f_fwd", E_REC), ("f_bwd", E_REC), ("i", E_REC), ("c_z", E_REC)):
    _SEC[_name] = (_off, _off + _size)
    _off += _size
GATES_OFF = _off


def _inproj_kernel(has_pos, d_model, tm, n_tiles, *refs):
    refs = list(refs)
    x_ref = refs.pop(0)
    pos_ref = refs.pop(0) if has_pos else None
    (mult_ref, shift_ref, w_ref, omlb_ref, cw_ref, cb_ref, lg_ref, lb_ref,
     a_ref, bu_ref, sbz_ref, q_ref, kf_ref, kb_ref, lf_ref, lbw_ref, v_ref, scz_ref, sg_ref,
     win_scr, ush_scr, saz_scr) = refs

    g = pl.program_id(0)

    @pl.when(g == 0)
    def _():
        win_scr[...] = jnp.zeros_like(win_scr)
        saz_scr[...] = jnp.zeros_like(saz_scr)

    x = x_ref[0]
    if has_pos:
        x = x + pos_ref[...]
    ms = jnp.mean(x * x, axis=-1, keepdims=True)
    h = (x * lax.rsqrt(ms + EPS) * mult_ref[0] + shift_ref[0]).astype(BF16)

    def proj(first, last):
        lo, hi = _SEC[first][0], _SEC[last][1]
        r = _dot(h, w_ref[:, lo:hi])
        return lambda name: r[:, _SEC[name][0] - lo:_SEC[name][1] - lo]

    starts = (g % n_tiles) == 0
    conv_p = proj("a_val", "a_z")
    u_new = conv_p("a_val") * _sigmoid(conv_p("a_gate"))
    saz_scr[g % 2] = _silu(conv_p("a_z")).astype(BF16)
    win_scr[HALO + tm:, :] = jnp.where(starts, 0.0, u_new[0:HALO])
    span = tm + 2 * HALO - SUBLANES
    for r in range(SUBLANES):
        ush_scr[r, 0:span, :] = win_scr[r:r + span, :]
    win_scr[0:HALO, :] = jnp.where(starts, 0.0, win_scr[tm:tm + HALO, :])
    win_scr[HALO:HALO + tm, :] = u_new

    def conv_rows(blk):
        base = blk * CONV_ROWS
        acc = jnp.broadcast_to(cb_ref[...], (CONV_ROWS, E_CONV))
        for j in range(CONV_WIDTH):
            off = j + HALO - CONV_WIDTH // 2
            lo = base + (off // SUBLANES) * SUBLANES
            tap = jnp.concatenate([cw_ref[j]] * (CONV_ROWS // SUBLANES), axis=0)
            acc = acc + tap * ush_scr[off % SUBLANES, lo:lo + CONV_ROWS, :]
        mu = jnp.mean(acc, axis=-1, keepdims=True)
        xc = acc - mu
        var = jnp.mean(xc * xc, axis=-1, keepdims=True)
        y = xc * lax.rsqrt(var + EPS) * lg_ref[...] + lb_ref[...]
        gate_prev = saz_scr[(g + 1) % 2, base:base + CONV_ROWS, :]
        a_ref[0, base:base + CONV_ROWS, :] = (_silu(y) * gate_prev.astype(F32)).astype(BF16)

    def sec_fourier():
        p = proj("b_u", "b_z")
        bu_ref[0] = p("b_u").astype(BF16)
        sbz_ref[0] = _silu(p("b_z")).astype(BF16)

    def sec_rec():
        p = proj("q", "c_z")
        q_ref[0] = p("q").astype(BF16)
        for name, k_ref, l_ref, row in (("f_fwd", kf_ref, lf_ref, 0), ("f_bwd", kb_ref, lbw_ref, 1)):
            k = jnp.minimum(omlb_ref[row:row + 1, :] * _sigmoid(-p(name)), K_MAX)
            k_ref[0] = k.astype(BF16)
            l_ref[0] = jnp.log(1.0 - k)
        v_ref[0] = p("i").astype(BF16)
        scz_ref[0] = _silu(p("c_z")).astype(BF16)

    def sec_gates():
        sg_ref[0] = _dot(h, w_ref[:, GATES_OFF:GATES_OFF + N_BRANCH * d_model]).astype(BF16)

    n_blk = tm // CONV_ROWS
    cuts = (n_blk // 4, (3 * n_blk) // 4, n_blk)
    for sec, lo, hi in ((sec_fourier, 0, cuts[0]), (sec_rec, cuts[0], cuts[1]), (sec_gates, cuts[1], cuts[2])):
        sec()
        for blk in range(lo, hi):
            conv_rows(blk)


def _inproj(x, pos, mult, shift, w_bf16, omlb, conv, tm):
    bsz, t, d = x.shape
    d_in = w_bf16.shape[1]
    has_pos = pos is not None
    n_tiles = t // tm
    n_all = bsz * n_tiles
    cur = lambda g: jnp.minimum(g, n_all - 1)
    prev = lambda g: jnp.maximum(g - 1, 0)
    tok = lambda w, at=cur: pl.BlockSpec((1, tm, w), lambda g: (at(g) // n_tiles, at(g) % n_tiles, 0))
    full = lambda arr: pl.BlockSpec(arr.shape, lambda g: (0,) * arr.ndim)
    in_specs = [tok(d)]
    args = [x]
    if has_pos:
        in_specs.append(pl.BlockSpec((tm, d), lambda g: (cur(g) % n_tiles, 0)))
        args.append(pos)
    in_specs += [pl.BlockSpec((1, 1, d), lambda g: (cur(g) // n_tiles, 0, 0)),
                 pl.BlockSpec((1, 1, d), lambda g: (cur(g) // n_tiles, 0, 0)),
                 pl.BlockSpec((d, d_in), lambda g: (0, 0), pipeline_mode=pl.Buffered(1)),
                 full(omlb)] + [full(c) for c in conv]
    args += [mult, shift, w_bf16, omlb, *conv]
    widths = [(E_FOURIER, BF16), (E_FOURIER, BF16),
              (E_REC, BF16), (E_REC, BF16), (E_REC, BF16), (E_REC, F32), (E_REC, F32),
              (E_REC, BF16), (E_REC, BF16), (N_BRANCH * d, BF16)]
    return pl.pallas_call(
        functools.partial(_inproj_kernel, has_pos, d, tm, n_tiles),
        grid=(n_all + 1,),
        in_specs=in_specs,
        out_specs=[tok(E_CONV, prev)] + [tok(w) for w, _ in widths],
        out_shape=[jax.ShapeDtypeStruct((bsz, t, w), dt) for w, dt in [(E_CONV, BF16)] + widths],
        scratch_shapes=[pltpu.VMEM((tm + 2 * HALO, E_CONV), F32),
                        pltpu.VMEM((SUBLANES, tm + 2 * HALO, E_CONV), F32),
                        pltpu.VMEM((2, tm, E_CONV), BF16)],
        compiler_params=_params("arbitrary"),
        name="inproj",
    )(*args)


def _fourier_kernel(t, scale, u_ref, gate_ref, cs_ref, cc_ref, o_ref, ab_scr):
    gw = E_FOURIER // FOURIER_GROUPS

    @pl.when(pl.program_id(1) == 0)
    def _():
        for g in range(FOURIER_GROUPS):
            r = _dot(u_ref[0, :, g * gw:(g + 1) * gw], cc_ref[...])
            ab_scr[0:t, g * gw:(g + 1) * gw] = r[:, :gw].astype(BF16)
            ab_scr[t:2 * t, g * gw:(g + 1) * gw] = r[:, gw:].astype(BF16)

    f = _dot(cs_ref[...], ab_scr[...])
    o_ref[0] = (f * scale * gate_ref[0].astype(F32)).astype(BF16)


def _dft_consts(t):
    gw = E_FOURIER // FOURIER_GROUPS
    jk = np.outer(np.arange(t), np.arange(t)) % t
    ang = 2.0 * np.pi * jk / t
    cs = np.concatenate([np.cos(ang), -np.sin(ang)], axis=1)
    jc = np.outer(np.arange(gw), np.arange(gw)) % gw
    angc = 2.0 * np.pi * jc / gw
    cc = np.concatenate([np.cos(angc), np.sin(angc)], axis=1)
    scale = 1.0 / np.sqrt(float(t) * gw)
    return jnp.asarray(cs, dtype=BF16), jnp.asarray(cc, dtype=BF16), float(scale)


def _fourier(bu, sbz, tr):
    bsz, t, e = bu.shape
    gw = e // FOURIER_GROUPS
    cs, cc, scale = _dft_consts(t)
    return pl.pallas_call(
        functools.partial(_fourier_kernel, t, scale),
        grid=(bsz, t // tr),
        in_specs=[pl.BlockSpec((1, t, e), lambda b, i: (b, 0, 0)),
                  pl.BlockSpec((1, tr, e), lambda b, i: (b, i, 0)),
                  pl.BlockSpec((tr, 2 * t), lambda b, i: (i, 0)),
                  pl.BlockSpec((gw, 2 * gw), lambda b, i: (0, 0))],
        out_specs=pl.BlockSpec((1, tr, e), lambda b, i: (b, i, 0)),
        out_shape=jax.ShapeDtypeStruct((bsz, t, e), BF16),
        scratch_shapes=[pltpu.VMEM((2 * t, e), BF16)],
        compiler_params=_params("parallel", "arbitrary"),
        name="fourier",
    )(bu, sbz, cs, cc)


def _level_ids():
    t = np.arange(CHUNK)[:, None]
    s = np.arange(CHUNK)[None, :]
    lv = np.full((CHUNK, CHUNK), -1, np.int32)
    size, level = CHUNK, int(np.log2(CHUNK // BASE))
    while size >= BASE:
        lv = np.where((t // size == s // size) & (s <= t), level, lv)
        size //= 2
        level -= 1
    return lv.astype(np.int32)


N_LEVELS = int(np.log2(CHUNK // BASE)) + 1
A_UNROLL = 8
STATIC_ITERS = 2
C_UNROLL = 4


def _chunk_rows(c):
    if isinstance(c, int):
        return pl.ds(c * CHUNK, CHUNK)
    return pl.ds(pl.multiple_of(c * CHUNK, CHUNK), CHUNK)


def _row(ref, r):
    return jnp.broadcast_to(ref[pl.ds(r, 1), :], (SUBLANES, HEAD))


def _level_factors(level, bc, bc_ref, q, k, fwd):
    groups = CHUNK // SUBLANES
    sub = lax.broadcasted_iota(jnp.int32, (CHUNK, HEAD), 0) % SUBLANES
    if level == 0:
        lo_row, hi_row = (0, BASE) if fwd else (BASE - 1, SUBLANES - 1)
        ref = jnp.concatenate(
            [jnp.where(sub[:SUBLANES] < BASE, _row(bc_ref, g * SUBLANES + lo_row),
                       _row(bc_ref, g * SUBLANES + hi_row)) for g in range(groups)], axis=0)
        e = bc - ref
        return (q * jnp.exp2(e)).astype(BF16), (k * jnp.exp2(-e)).astype(BF16), [slice(0, CHUNK)]
    half = BASE << (level - 1)
    if half < SUBLANES:
        ref_row = half - 1 if fwd else half
        ref = jnp.concatenate([_row(bc_ref, g * SUBLANES + ref_row) for g in range(groups)], axis=0)
        q_rows = (sub >= half) if fwd else (sub < half)
        ex = jnp.exp2(jnp.where(q_rows, bc - ref, ref - bc))
        zero = jnp.zeros_like(ex)
        return (jnp.where(q_rows, q * ex, zero).astype(BF16),
                jnp.where(q_rows, zero, k * ex).astype(BF16), [slice(0, CHUNK)])
    qs, ks, q_rows = [], [], []
    zero = jnp.zeros((half, HEAD), F32)
    for blk in range(CHUNK // (2 * half)):
        a = blk * 2 * half
        lo, hi = slice(a, a + half), slice(a + half, a + 2 * half)
        ref = jnp.broadcast_to(bc_ref[pl.ds(a + half - 1 if fwd else a + half, 1), :], (half, HEAD))
        if fwd:
            qs.append(q[hi] * jnp.exp2(bc[hi] - ref))
            ks += [k[lo] * jnp.exp2(ref - bc[lo]), zero]
            q_rows.append(hi)
        else:
            qs.append(q[lo] * jnp.exp2(bc[lo] - ref))
            ks += [zero, k[hi] * jnp.exp2(ref - bc[hi])]
            q_rows.append(lo)
    return jnp.concatenate(qs, axis=0).astype(BF16), jnp.concatenate(ks, axis=0).astype(BF16), q_rows


def _split2(g):
    hi = g.astype(BF16)
    lo = (g - hi.astype(F32)).astype(BF16)
    return jnp.concatenate([hi, lo], axis=1)


def _rec_kernel(n_ctx, n_lat, ctx_out, *refs):
    (qx, kfx, kbx, lfx, lbx, vx, zx, qc, kfc, kbc, lfc, lbc, vc, zc,
     tril_ref, triu_ref, lvf_ref, lvb_ref, g_ref) = refs[:19]
    if ctx_out:
        ox_ref, oc_ref = refs[19:21]
        scr = refs[21:]
    else:
        ox_ref, oc_ref = refs[19], None
        scr = refs[20:]
    bcf_scr, bcb_scr, qfb_scr, oin_scr, dst_scr, a_scr, st_scr = scr
    kw = HEAD

    def chunks_a(q_ref, kf_ref, kb_ref, lf_ref, lb_ref, v_ref, chunks, slot0):
        dirs = ((True, kf_ref, lf_ref, tril_ref, lvf_ref, bcf_scr),
                (False, kb_ref, lb_ref, triu_ref, lvb_ref, bcb_scr))
        rows = [_chunk_rows(c) for c in chunks]
        units = [(u, d) for u in range(len(chunks)) for d in range(2)]
        q = [q_ref[0, r, :].astype(F32) for r in rows]
        v = [v_ref[0, r, :] for r in rows]
        k = {(u, d): dirs[d][1][0, rows[u], :].astype(F32) for u, d in units}
        cs = {(u, d): _dot(dirs[d][3][...], _split2(dirs[d][2][0, rows[u], :])) for u, d in units}
        bc, bref = {}, {}
        for u, d in units:
            bc[u, d] = (cs[u, d][:, :kw] + cs[u, d][:, kw:]) * LOG2_E
            bref[u, d] = dirs[d][5].at[u]
            bref[u, d][...] = bc[u, d]
        qi, kd, dec = {}, {}, {}
        for u, d in units:
            b_edge = bref[u, d][pl.ds(CHUNK - 1 if d == 0 else 0, 1), :]
            qi[u, d] = (q[u] * jnp.exp2(bc[u, d])).astype(BF16)
            kd[u, d] = (k[u, d] * jnp.exp2(b_edge - bc[u, d])).astype(BF16)
            dec[u, d] = jnp.exp2(b_edge)
        groups = CHUNK // SUBLANES
        sc = {ud: [jnp.zeros((SUBLANES, CHUNK), F32) for _ in range(groups)] for ud in units}
        for level in range(N_LEVELS):
            for u, d in units:
                qt, kt, q_rows = _level_factors(level, bc[u, d], bref[u, d], q[u], k[u, d], d == 0)
                p = _dot_nt(qt, kt)
                at = 0
                for sl in q_rows:
                    for g in range(sl.start // SUBLANES, sl.stop // SUBLANES):
                        own = dirs[d][4][g * SUBLANES:(g + 1) * SUBLANES, :] == level
                        sc[u, d][g] = jnp.where(own, p[at:at + SUBLANES], sc[u, d][g])
                        at += SUBLANES
        for u in range(len(chunks)):
            slot = slot0 + chunks[u]
            scores = (jnp.concatenate(sc[u, 0], axis=0) + jnp.concatenate(sc[u, 1], axis=0)).astype(BF16)
            oin_scr[slot] = _dot(scores, v[u])
            qfb_scr[slot] = jnp.concatenate([qi[u, 0], qi[u, 1]], axis=1)
            dst_scr[slot] = _dot_tn(v[u], jnp.concatenate([kd[u, 0], kd[u, 1]], axis=1))
            a_scr[slot] = jnp.concatenate([dec[u, 0], dec[u, 1]], axis=1)

    def chunk_c(z_ref, o_ref, c, slot):
        rows = _chunk_rows(c)
        o = oin_scr[slot] + _dot_nt(qfb_scr[slot], st_scr[slot])
        o = o * lax.rsqrt(jnp.mean(o * o, axis=-1, keepdims=True) + EPS)
        o_ref[0, rows, :] = (o * g_ref[...] * z_ref[0, rows, :].astype(F32)).astype(BF16)

    def for_chunks(n, per_iter, fn):
        per_iter = max(u for u in range(1, per_iter + 1) if n % u == 0)

        def body(i, carry):
            fn([i * per_iter + u for u in range(per_iter)])
            return carry

        if n // per_iter <= STATIC_ITERS:
            for i in range(n // per_iter):
                body(i, 0)
        else:
            lax.fori_loop(0, n // per_iter, body, 0)

    for_chunks(n_ctx, A_UNROLL, lambda cs: chunks_a(qc, kfc, kbc, lfc, lbc, vc, cs, 0))
    for_chunks(n_lat, A_UNROLL, lambda cs: chunks_a(qx, kfx, kbx, lfx, lbx, vx, cs, n_ctx))

    n_all = n_ctx + n_lat
    fwd_order = list(range(n_all))
    bwd_order = list(range(n_ctx - 1, -1, -1)) + list(range(n_all - 1, n_ctx - 1, -1))
    for order, cols in ((fwd_order, slice(0, kw)), (bwd_order, slice(kw, 2 * kw))):
        s = jnp.zeros((kw, kw), F32)
        for slot in order:
            st_scr[slot, :, cols] = s.astype(BF16)
            s = a_scr[slot, :, cols] * s + dst_scr[slot, :, cols]

    for_chunks(n_lat, C_UNROLL, lambda cs: [chunk_c(zx, ox_ref, c, c + n_ctx) for c in cs])
    if ctx_out:
        for_chunks(n_ctx, C_UNROLL, lambda cs: [chunk_c(zc, oc_ref, c, c) for c in cs])


def _recurrence(lat, ctx, rec_g, ctx_out):
    bsz, t, _ = lat[0].shape
    tc = ctx[0].shape[1]
    n_lat, n_ctx = t // CHUNK, tc // CHUNK
    n_all = n_lat + n_ctx
    lvf = _level_ids()
    tri = np.tril(np.ones((CHUNK, CHUNK), np.float32))
    consts = [jnp.asarray(tri, dtype=BF16), jnp.asarray(tri.T, dtype=BF16),
              jnp.asarray(lvf), jnp.asarray(lvf.T)]
    head = lambda n: pl.BlockSpec((1, n, HEAD), lambda b, h: (b, 0, h))
    const = pl.BlockSpec((CHUNK, CHUNK), lambda b, h: (0, 0))
    in_specs = ([head(t)] * 7 + [head(tc)] * 7 + [const] * 4
                + [pl.BlockSpec((1, HEAD), lambda b, h: (0, h))])
    out_specs = [head(t)]
    out_shape = [jax.ShapeDtypeStruct((bsz, t, E_REC), BF16)]
    if ctx_out:
        out_specs.append(head(tc))
        out_shape.append(jax.ShapeDtypeStruct((bsz, tc, E_REC), BF16))
    scratch = [pltpu.VMEM((A_UNROLL, CHUNK, HEAD), F32), pltpu.VMEM((A_UNROLL, CHUNK, HEAD), F32),
               pltpu.VMEM((n_all, CHUNK, 2 * HEAD), BF16),
               pltpu.VMEM((n_all, CHUNK, HEAD), F32),
               pltpu.VMEM((n_all, HEAD, 2 * HEAD), F32),
               pltpu.VMEM((n_all, 1, 2 * HEAD), F32),
               pltpu.VMEM((n_all, HEAD, 2 * HEAD), BF16)]
    out = pl.pallas_call(
        functools.partial(_rec_kernel, n_ctx, n_lat, ctx_out),
        grid=(bsz, REC_HEADS),
        in_specs=in_specs,
        out_specs=out_specs,
        out_shape=out_shape,
        scratch_shapes=scratch,
        compiler_params=_params("parallel", "parallel"),
        name="recurrence",
    )(*lat, *ctx, *consts, rec_g.reshape(1, E_REC))
    return out if ctx_out else (out[0], None)


def _mix_kernel(has_pos, last, d_model, *refs):
    refs = list(refs)
    a_ref, fb_ref, oc_ref, sg_ref, x_ref = refs[:5]
    refs = refs[5:]
    pos_ref = refs.pop(0) if has_pos else None
    gate_ref, wpa_ref, wpb_ref, wpc_ref, wo_ref = refs[:5]
    refs = refs[5:]
    fg_ref = refs.pop(0) if last else None
    (o_ref,) = refs

    ya = _dot(a_ref[0], wpa_ref[...])
    yb = _dot(fb_ref[0], wpb_ref[...])
    yc = _dot(oc_ref[0], wpc_ref[...])
    d = d_model
    merge = lambda j: _sigmoid(sg_ref[0, :, j * d:(j + 1) * d].astype(F32))
    y = merge(0) * ya + merge(1) * yb + merge(2) * yc
    z = _dot(y.astype(BF16), wo_ref[...])
    x = x_ref[0]
    if has_pos:
        x = x + pos_ref[...]
    xn = x + gate_ref[0] * z
    if last:
        xn = xn * lax.rsqrt(jnp.mean(xn * xn, axis=-1, keepdims=True) + EPS) * fg_ref[...]
    o_ref[0] = xn


def _mix(a, fb, oc, sg, x, pos, gate, wpa, wpb, wpc, wo, final_g, tm):
    bsz, t, d = x.shape
    has_pos = pos is not None
    last = final_g is not None
    tok = lambda w: pl.BlockSpec((1, tm, w), lambda b, i: (b, i, 0))
    full = lambda arr: pl.BlockSpec(arr.shape, lambda b, i: (0,) * arr.ndim)
    in_specs = [tok(E_CONV), tok(E_FOURIER), tok(E_REC), tok(N_BRANCH * d), tok(d)]
    args = [a, fb, oc, sg, x]
    if has_pos:
        in_specs.append(pl.BlockSpec((tm, d), lambda b, i: (i, 0)))
        args.append(pos)
    params = [wpa, wpb, wpc, wo]
    in_specs += [pl.BlockSpec((1, 1, d), lambda b, i: (b, 0, 0))] + [full(p) for p in params]
    args += [gate] + params
    if last:
        in_specs.append(full(final_g))
        args.append(final_g)
    return pl.pallas_call(
        functools.partial(_mix_kernel, has_pos, last, d),
        grid=(bsz, t // tm),
        in_specs=in_specs,
        out_specs=tok(d),
        out_shape=jax.ShapeDtypeStruct((bsz, t, d), F32),
        compiler_params=_params("parallel", "parallel"),
        name="mix",
    )(*args)


def _sincos_2d(rows, cols, d):
    quarter = d // 4
    omega = 1.0 / (10000.0 ** (jnp.arange(quarter, dtype=F32) / quarter))
    er = jnp.arange(rows, dtype=F32)[:, None] * omega
    ec = jnp.arange(cols, dtype=F32)[:, None] * omega
    emb_r = jnp.concatenate([jnp.sin(er), jnp.cos(er)], axis=-1)
    emb_c = jnp.concatenate([jnp.sin(ec), jnp.cos(ec)], axis=-1)
    emb = jnp.concatenate([jnp.broadcast_to(emb_r[:, None, :], (rows, cols, d // 2)),
                           jnp.broadcast_to(emb_c[None, :, :], (rows, cols, d // 2))], axis=-1)
    return emb.reshape(rows * cols, d)


def _tile(n, want):
    return want if n % want == 0 else n


def kernel(x, c, ctx, c_ctx, w_ada, b_ada, norm_g, w_in, conv_w, conv_b, conv_ln_g, conv_ln_b,
           rec_lb, rec_norm_g, w_pa, w_pb, w_pc, w_out, final_g):
    bsz, t, d = x.shape
    tc = ctx.shape[1]
    depth = w_in.shape[0]
    assert t % CHUNK == 0 and tc % CHUNK == 0 and t % GRID_W == 0

    pos = _sincos_2d(t // GRID_W, GRID_W, d)
    lb_soft = jax.nn.softmax(rec_lb.astype(F32), axis=1)
    lbs = jnp.cumsum(lb_soft, axis=1) - lb_soft[:, :1]

    rows = -(-(bsz + 1) // SUBLANES) * SUBLANES
    cc = jnp.zeros((rows, d), F32).at[:bsz].set(c).at[bsz].set(c_ctx)
    mod = _ada(cc, w_ada, b_ada)

    tm_x, tm_c = _tile(t, 256), _tile(tc, 256)
    for l in range(depth):
        last = l == depth - 1
        shift, scale, gate = mod[l, :, :d], mod[l, :, d:2 * d], mod[l, :, 2 * d:]
        mult = norm_g[l][None, :] * (1.0 + scale)
        mult_x, shift_x, gate_x = (a[:bsz, None, :] for a in (mult, shift, gate))
        mult_c, shift_c, gate_c = (jnp.broadcast_to(a[bsz][None, None, :], (bsz, 1, d))
                                   for a in (mult, shift, gate))
        w_l = w_in[l].astype(BF16)
        omlb = 1.0 - lbs[:, l, :]
        conv = (jnp.broadcast_to(conv_w[l].reshape(CONV_WIDTH, 1, E_CONV), (CONV_WIDTH, SUBLANES, E_CONV)),
                conv_b[l][None], conv_ln_g[l][None], conv_ln_b[l][None])

        px = _inproj(x, pos if l == 0 else None, mult_x, shift_x, w_l, omlb, conv, tm_x)
        pc = _inproj(ctx, None, mult_c, shift_c, w_l, omlb, conv, tm_c)

        ox, oc = _recurrence(px[3:10], pc[3:10], rec_norm_g[l], ctx_out=not last)

        mixw = (w_pa[l].astype(BF16), w_pb[l].astype(BF16), w_pc[l].astype(BF16), w_out[l].astype(BF16))
        fbx = _fourier(px[1], px[2], _tile(t, 1024))
        x = _mix(px[0], fbx, ox, px[10], x, pos if l == 0 else None, gate_x, *mixw,
                 final_g[None] if last else None, _tile(t, 512))
        if not last:
            fbc = _fourier(pc[1], pc[2], _tile(tc, 512))
            ctx = _mix(pc[0], fbc, oc, pc[10], ctx, None, gate_c, *mixw, None, tm_c)
    return x
```

```python
import functools

import numpy as np
import jax
import jax.numpy as jnp
from jax import lax
from jax.experimental import pallas as pl
from jax.experimental.pallas import tpu as pltpu

F32 = jnp.float32
BF16 = jnp.bfloat16

E_CONV = 768
CONV_WIDTH = 31
E_FOURIER = 512
FOURIER_GROUPS = 4
E_REC = 768
REC_HEADS = 6
HEAD = E_REC // REC_HEADS
N_BRANCH = 3
EPS = 1e-6
K_MAX = 1.0 - 1e-6
LOG2_E = 1.4426950408889634
GRID_W = 64

CHUNK = 128
BASE = 4
SUBLANES = 8
HALO = 16
CONV_ROWS = 32
VMEM_LIMIT = 56 * 1024 * 1024


def _sigmoid(x):
    return 0.5 * jnp.tanh(0.5 * x) + 0.5


def _silu(x):
    return x * _sigmoid(x)


def _dot(a, b):
    return jnp.dot(a, b, preferred_element_type=F32)


def _dot_nt(a, b):
    return lax.dot_general(a, b, (((1,), (1,)), ((), ())), preferred_element_type=F32)


def _dot_tn(a, b):
    return lax.dot_general(a, b, (((0,), (0,)), ((), ())), preferred_element_type=F32)


def _params(*sem):
    return pltpu.CompilerParams(dimension_semantics=sem, vmem_limit_bytes=VMEM_LIMIT)


def _ada_kernel(cc_ref, w_ref, b_ref, o_ref):
    s = _silu(cc_ref[...])
    o_ref[0] = jnp.dot(s, w_ref[0], preferred_element_type=F32,
                       precision=lax.Precision.HIGHEST) + b_ref[0]


def _ada(cc, w_ada, b_ada):
    depth, d, d3 = w_ada.shape
    rows = cc.shape[0]
    return pl.pallas_call(
        _ada_kernel,
        grid=(depth, d3 // d),
        in_specs=[pl.BlockSpec((rows, d), lambda l, j: (0, 0)),
                  pl.BlockSpec((1, d, d), lambda l, j: (l, 0, j)),
                  pl.BlockSpec((1, 1, d), lambda l, j: (l, 0, j))],
        out_specs=pl.BlockSpec((1, rows, d), lambda l, j: (l, 0, j)),
        out_shape=jax.ShapeDtypeStruct((depth, rows, d3), F32),
        compiler_params=_params("arbitrary", "arbitrary"),
        name="ada",
    )(cc, w_ada, b_ada.reshape(depth, 1, d3))


_SEC = {}
_off = 0
for _name, _size in (("a_val", E_CONV), ("a_gate", E_CONV), ("a_z", E_CONV),
                     ("b_u", E_FOURIER), ("b_z", E_FOURIER),
                     ("q", E_REC), ("f_fwd", E_REC), ("f_bwd", E_REC), ("i", E_REC), ("c_z", E_REC)):
    _SEC[_name] = (_off, _off + _size)
    _off += _size
GATES_OFF = _off


def _inproj_kernel(has_pos, d_model, tm, n_tiles, *refs):
    refs = list(refs)
    x_ref = refs.pop(0)
    pos_ref = refs.pop(0) if has_pos else None
    (mult_ref, shift_ref, w_ref, omlb_ref, cw_ref, cb_ref, lg_ref, lb_ref,
     a_ref, bu_ref, sbz_ref, q_ref, kf_ref, kb_ref, lf_ref, lbw_ref, v_ref, scz_ref, sg_ref,
     win_scr, ush_scr, saz_scr) = refs

    g = pl.program_id(0)

    @pl.when(g == 0)
    def _():
        win_scr[...] = jnp.zeros_like(win_scr)
        saz_scr[...] = jnp.zeros_like(saz_scr)

    x = x_ref[0]
    if has_pos:
        x = x + pos_ref[...]
    ms = jnp.mean(x * x, axis=-1, keepdims=True)
    h = (x * lax.rsqrt(ms + EPS) * mult_ref[0] + shift_ref[0]).astype(BF16)

    def proj(first, last):
        lo, hi = _SEC[first][0], _SEC[last][1]
        r = _dot(h, w_ref[:, lo:hi])
        return lambda name: r[:, _SEC[name][0] - lo:_SEC[name][1] - lo]

    starts = (g % n_tiles) == 0
    conv_p = proj("a_val", "a_z")
    u_new = conv_p("a_val") * _sigmoid(conv_p("a_gate"))
    saz_scr[g % 2] = _silu(conv_p("a_z")).astype(BF16)
    win_scr[HALO + tm:, :] = jnp.where(starts, 0.0, u_new[0:HALO])
    span = tm + 2 * HALO - SUBLANES
    for r in range(SUBLANES):
        ush_scr[r, 0:span, :] = win_scr[r:r + span, :]
    win_scr[0:HALO, :] = jnp.where(starts, 0.0, win_scr[tm:tm + HALO, :])
    win_scr[HALO:HALO + tm, :] = u_new

    def conv_rows(blk):
        base = blk * CONV_ROWS
        acc = jnp.broadcast_to(cb_ref[...], (CONV_ROWS, E_CONV))
        for j in range(CONV_WIDTH):
            off = j + HALO - CONV_WIDTH // 2
            lo = base + (off // SUBLANES) * SUBLANES
            tap = jnp.concatenate([cw_ref[j]] * (CONV_ROWS // SUBLANES), axis=0)
            acc = acc + tap * ush_scr[off % SUBLANES, lo:lo + CONV_ROWS, :]
        mu = jnp.mean(acc, axis=-1, keepdims=True)
        xc = acc - mu
        var = jnp.mean(xc * xc, axis=-1, keepdims=True)
        y = xc * lax.rsqrt(var + EPS) * lg_ref[...] + lb_ref[...]
        gate_prev = saz_scr[(g + 1) % 2, base:base + CONV_ROWS, :]
        a_ref[0, base:base + CONV_ROWS, :] = (_silu(y) * gate_prev.astype(F32)).astype(BF16)

    def sec_fourier():
        p = proj("b_u", "b_z")
        bu_ref[0] = p("b_u").astype(BF16)
        sbz_ref[0] = _silu(p("b_z")).astype(BF16)

    def sec_rec():
        p = proj("q", "c_z")
        q_ref[0] = p("q").astype(BF16)
        for name, k_ref, l_ref, row in (("f_fwd", kf_ref, lf_ref, 0), ("f_bwd", kb_ref, lbw_ref, 1)):
            k = jnp.minimum(omlb_ref[row:row + 1, :] * _sigmoid(-p(name)), K_MAX)
            k_ref[0] = k.astype(BF16)
            l_ref[0] = jnp.log(1.0 - k)
        v_ref[0] = p("i").astype(BF16)
        scz_ref[0] = _silu(p("c_z")).astype(BF16)

    def sec_gates():
        sg_ref[0] = _dot(h, w_ref[:, GATES_OFF:GATES_OFF + N_BRANCH * d_model]).astype(BF16)

    n_blk = tm // CONV_ROWS
    cuts = (n_blk // 4, (3 * n_blk) // 4, n_blk)
    for sec, lo, hi in ((sec_fourier, 0, cuts[0]), (sec_rec, cuts[0], cuts[1]), (sec_gates, cuts[1], cuts[2])):
        sec()
        for blk in range(lo, hi):
            conv_rows(blk)


def _inproj(x, pos, mult, shift, w_bf16, omlb, conv, tm):
    bsz, t, d = x.shape
    d_in = w_bf16.shape[1]
    has_pos = pos is not None
    n_tiles = t // tm
    n_all = bsz * n_tiles
    cur = lambda g: jnp.minimum(g, n_all - 1)
    prev = lambda g: jnp.maximum(g - 1, 0)
    tok = lambda w, at=cur: pl.BlockSpec((1, tm, w), lambda g: (at(g) // n_tiles, at(g) % n_tiles, 0))
    full = lambda arr: pl.BlockSpec(arr.shape, lambda g: (0,) * arr.ndim)
    in_specs = [tok(d)]
    args = [x]
    if has_pos:
        in_specs.append(pl.BlockSpec((tm, d), lambda g: (cur(g) % n_tiles, 0)))
        args.append(pos)
    in_specs += [pl.BlockSpec((1, 1, d), lambda g: (cur(g) // n_tiles, 0, 0)),
                 pl.BlockSpec((1, 1, d), lambda g: (cur(g) // n_tiles, 0, 0)),
                 pl.BlockSpec((d, d_in), lambda g: (0, 0), pipeline_mode=pl.Buffered(1)),
                 full(omlb)] + [full(c) for c in conv]
    args += [mult, shift, w_bf16, omlb, *conv]
    widths = [(E_FOURIER, BF16), (E_FOURIER, BF16),
              (E_REC, BF16), (E_REC, BF16), (E_REC, BF16), (E_REC, F32), (E_REC, F32),
              (E_REC, BF16), (E_REC, BF16), (N_BRANCH * d, BF16)]
    return pl.pallas_call(
        functools.partial(_inproj_kernel, has_pos, d, tm, n_tiles),
        grid=(n_all + 1,),
        in_specs=in_specs,
        out_specs=[tok(E_CONV, prev)] + [tok(w) for w, _ in widths],
        out_shape=[jax.ShapeDtypeStruct((bsz, t, w), dt) for w, dt in [(E_CONV, BF16)] + widths],
        scratch_shapes=[pltpu.VMEM((tm + 2 * HALO, E_CONV), F32),
                        pltpu.VMEM((SUBLANES, tm + 2 * HALO, E_CONV), F32),
                        pltpu.VMEM((2, tm, E_CONV), BF16)],
        compiler_params=_params("arbitrary"),
        name="inproj",
    )(*args)


def _fourier_kernel(t, scale, u_ref, gate_ref, cs_ref, cc_ref, o_ref, ab_scr):
    gw = E_FOURIER // FOURIER_GROUPS

    @pl.when(pl.program_id(1) == 0)
    def _():
        for g in range(FOURIER_GROUPS):
            r = _dot(u_ref[0, :, g * gw:(g + 1) * gw], cc_ref[...])
            ab_scr[0:t, g * gw:(g + 1) * gw] = r[:, :gw].astype(BF16)
            ab_scr[t:2 * t, g * gw:(g + 1) * gw] = r[:, gw:].astype(BF16)

    f = _dot(cs_ref[...], ab_scr[...])
    o_ref[0] = (f * scale * gate_ref[0].astype(F32)).astype(BF16)


def _dft_consts(t):
    gw = E_FOURIER // FOURIER_GROUPS
    jk = np.outer(np.arange(t), np.arange(t)) % t
    ang = 2.0 * np.pi * jk / t
    cs = np.concatenate([np.cos(ang), -np.sin(ang)], axis=1)
    jc = np.outer(np.arange(gw), np.arange(gw)) % gw
    angc = 2.0 * np.pi * jc / gw
    cc = np.concatenate([np.cos(angc), np.sin(angc)], axis=1)
    scale = 1.0 / np.sqrt(float(t) * gw)
    return jnp.asarray(cs, dtype=BF16), jnp.asarray(cc, dtype=BF16), float(scale)


def _fourier(bu, sbz, tr):
    bsz, t, e = bu.shape
    gw = e // FOURIER_GROUPS
    cs, cc, scale = _dft_consts(t)
    return pl.pallas_call(
        functools.partial(_fourier_kernel, t, scale),
        grid=(bsz, t // tr),
        in_specs=[pl.BlockSpec((1, t, e), lambda b, i: (b, 0, 0)),
                  pl.BlockSpec((1, tr, e), lambda b, i: (b, i, 0)),
                  pl.BlockSpec((tr, 2 * t), lambda b, i: (i, 0)),
                  pl.BlockSpec((gw, 2 * gw), lambda b, i: (0, 0))],
        out_specs=pl.BlockSpec((1, tr, e), lambda b, i: (b, i, 0)),
        out_shape=jax.ShapeDtypeStruct((bsz, t, e), BF16),
        scratch_shapes=[pltpu.VMEM((2 * t, e), BF16)],
        compiler_params=_params("parallel", "arbitrary"),
        name="fourier",
    )(bu, sbz, cs, cc)


def _level_ids():
    t = np.arange(CHUNK)[:, None]
    s = np.arange(CHUNK)[None, :]
    lv = np.full((CHUNK, CHUNK), -1, np.int32)
    size, level = CHUNK, int(np.log2(CHUNK // BASE))
    while size >= BASE:
        lv = np.where((t // size == s // size) & (s <= t), level, lv)
        size //= 2
        level -= 1
    return lv.astype(np.int32)


N_LEVELS = int(np.log2(CHUNK // BASE)) + 1
A_UNROLL = 8
STATIC_ITERS = 2
C_UNROLL = 8


def _chunk_rows(c):
    if isinstance(c, int):
        return pl.ds(c * CHUNK, CHUNK)
    return pl.ds(pl.multiple_of(c * CHUNK, CHUNK), CHUNK)


def _row(ref, r):
    return jnp.broadcast_to(ref[pl.ds(r, 1), :], (SUBLANES, HEAD))


def _level_factors(level, bc, bc_ref, q, k, fwd):
    groups = CHUNK // SUBLANES
    sub = lax.broadcasted_iota(jnp.int32, (CHUNK, HEAD), 0) % SUBLANES
    if level == 0:
        lo_row, hi_row = (0, BASE) if fwd else (BASE - 1, SUBLANES - 1)
        ref = jnp.concatenate(
            [jnp.where(sub[:SUBLANES] < BASE, _row(bc_ref, g * SUBLANES + lo_row),
                       _row(bc_ref, g * SUBLANES + hi_row)) for g in range(groups)], axis=0)
        e = bc - ref
        return (q * jnp.exp2(e)).astype(BF16), (k * jnp.exp2(-e)).astype(BF16), [slice(0, CHUNK)]
    half = BASE << (level - 1)
    if half < SUBLANES:
        ref_row = half - 1 if fwd else half
        ref = jnp.concatenate([_row(bc_ref, g * SUBLANES + ref_row) for g in range(groups)], axis=0)
        q_rows = (sub >= half) if fwd else (sub < half)
        ex = jnp.exp2(jnp.where(q_rows, bc - ref, ref - bc))
        zero = jnp.zeros_like(ex)
        return (jnp.where(q_rows, q * ex, zero).astype(BF16),
                jnp.where(q_rows, zero, k * ex).astype(BF16), [slice(0, CHUNK)])
    qs, ks, q_rows = [], [], []
    zero = jnp.zeros((half, HEAD), F32)
    for blk in range(CHUNK // (2 * half)):
        a = blk * 2 * half
        lo, hi = slice(a, a + half), slice(a + half, a + 2 * half)
        ref = jnp.broadcast_to(bc_ref[pl.ds(a + half - 1 if fwd else a + half, 1), :], (half, HEAD))
        if fwd:
            qs.append(q[hi] * jnp.exp2(bc[hi] - ref))
            ks += [k[lo] * jnp.exp2(ref - bc[lo]), zero]
            q_rows.append(hi)
        else:
            qs.append(q[lo] * jnp.exp2(bc[lo] - ref))
            ks += [zero, k[hi] * jnp.exp2(ref - bc[hi])]
            q_rows.append(lo)
    return jnp.concatenate(qs, axis=0).astype(BF16), jnp.concatenate(ks, axis=0).astype(BF16), q_rows


def _split2(g):
    hi = g.astype(BF16)
    lo = (g - hi.astype(F32)).astype(BF16)
    return jnp.concatenate([hi, lo], axis=1)


def _rec_kernel(n_ctx, n_lat, ctx_out, *refs):
    (qx, kfx, kbx, lfx, lbx, vx, zx, qc, kfc, kbc, lfc, lbc, vc, zc,
     tril_ref, triu_ref, lvf_ref, lvb_ref, g_ref) = refs[:19]
    if ctx_out:
        ox_ref, oc_ref = refs[19:21]
        scr = refs[21:]
    else:
        ox_ref, oc_ref = refs[19], None
        scr = refs[20:]
    bcf_scr, bcb_scr, qfb_scr, oin_scr, dst_scr, a_scr, st_scr = scr
    kw = HEAD

    def chunks_a(q_ref, kf_ref, kb_ref, lf_ref, lb_ref, v_ref, chunks, slot0, outputs):
        dirs = ((True, kf_ref, lf_ref, tril_ref, lvf_ref, bcf_scr),
                (False, kb_ref, lb_ref, triu_ref, lvb_ref, bcb_scr))
        rows = [_chunk_rows(c) for c in chunks]
        units = [(u, d) for u in range(len(chunks)) for d in range(2)]
        q = [q_ref[0, r, :].astype(F32) for r in rows] if outputs else None
        v = [v_ref[0, r, :] for r in rows]
        k = {(u, d): dirs[d][1][0, rows[u], :].astype(F32) for u, d in units}
        cs = {(u, d): _dot(dirs[d][3][...], _split2(dirs[d][2][0, rows[u], :])) for u, d in units}
        bc, bref = {}, {}
        for u, d in units:
            bc[u, d] = (cs[u, d][:, :kw] + cs[u, d][:, kw:]) * LOG2_E
            bref[u, d] = dirs[d][5].at[u]
            bref[u, d][...] = bc[u, d]
        qi, kd, dec = {}, {}, {}
        for u, d in units:
            b_edge = bref[u, d][pl.ds(CHUNK - 1 if d == 0 else 0, 1), :]
            if outputs:
                qi[u, d] = (q[u] * jnp.exp2(bc[u, d])).astype(BF16)
            kd[u, d] = (k[u, d] * jnp.exp2(b_edge - bc[u, d])).astype(BF16)
            dec[u, d] = jnp.exp2(b_edge)
        groups = CHUNK // SUBLANES
        sc = {ud: [jnp.zeros((SUBLANES, CHUNK), F32) for _ in range(groups)] for ud in units}
        for level in range(N_LEVELS if outputs else 0):
            for u, d in units:
                qt, kt, q_rows = _level_factors(level, bc[u, d], bref[u, d], q[u], k[u, d], d == 0)
                p = _dot_nt(qt, kt)
                at = 0
                for sl in q_rows:
                    for g in range(sl.start // SUBLANES, sl.stop // SUBLANES):
                        own = dirs[d][4][g * SUBLANES:(g + 1) * SUBLANES, :] == level
                        sc[u, d][g] = jnp.where(own, p[at:at + SUBLANES], sc[u, d][g])
                        at += SUBLANES
        for u in range(len(chunks)):
            slot = slot0 + chunks[u]
            if outputs:
                scores = (jnp.concatenate(sc[u, 0], axis=0) + jnp.concatenate(sc[u, 1], axis=0)).astype(BF16)
                oin_scr[slot] = _dot(scores, v[u])
                qfb_scr[slot] = jnp.concatenate([qi[u, 0], qi[u, 1]], axis=1)
            dst_scr[slot] = _dot_tn(v[u], jnp.concatenate([kd[u, 0], kd[u, 1]], axis=1))
            a_scr[slot] = jnp.concatenate([dec[u, 0], dec[u, 1]], axis=1)

    def chunk_c(z_ref, o_ref, c, slot):
        rows = _chunk_rows(c)
        o = oin_scr[slot] + _dot_nt(qfb_scr[slot], st_scr[slot])
        o = o * lax.rsqrt(jnp.mean(o * o, axis=-1, keepdims=True) + EPS)
        o_ref[0, rows, :] = (o * g_ref[...] * z_ref[0, rows, :].astype(F32)).astype(BF16)

    def for_chunks(n, per_iter, fn):
        per_iter = max(u for u in range(1, per_iter + 1) if n % u == 0)

        def body(i, carry):
            fn([i * per_iter + u for u in range(per_iter)])
            return carry

        if n // per_iter <= STATIC_ITERS:
            for i in range(n // per_iter):
                body(i, 0)
        else:
            lax.fori_loop(0, n // per_iter, body, 0)

    for_chunks(n_ctx, A_UNROLL, lambda cs: chunks_a(qc, kfc, kbc, lfc, lbc, vc, cs, 0, ctx_out))
    for_chunks(n_lat, A_UNROLL, lambda cs: chunks_a(qx, kfx, kbx, lfx, lbx, vx, cs, n_ctx, True))

    n_all = n_ctx + n_lat
    fwd_order = list(range(n_all))
    bwd_order = list(range(n_ctx - 1, -1, -1)) + list(range(n_all - 1, n_ctx - 1, -1))
    for order, cols in ((fwd_order, slice(0, kw)), (bwd_order, slice(kw, 2 * kw))):
        s = jnp.zeros((kw, kw), F32)
        for slot in order:
            st_scr[slot, :, cols] = s.astype(BF16)
            s = a_scr[slot, :, cols] * s + dst_scr[slot, :, cols]

    for_chunks(n_lat, C_UNROLL, lambda cs: [chunk_c(zx, ox_ref, c, c + n_ctx) for c in cs])
    if ctx_out:
        for_chunks(n_ctx, C_UNROLL, lambda cs: [chunk_c(zc, oc_ref, c, c) for c in cs])


def _recurrence(lat, ctx, rec_g, ctx_out):
    bsz, t, _ = lat[0].shape
    tc = ctx[0].shape[1]
    n_lat, n_ctx = t // CHUNK, tc // CHUNK
    n_all = n_lat + n_ctx
    lvf = _level_ids()
    tri = np.tril(np.ones((CHUNK, CHUNK), np.float32))
    consts = [jnp.asarray(tri, dtype=BF16), jnp.asarray(tri.T, dtype=BF16),
              jnp.asarray(lvf), jnp.asarray(lvf.T)]
    head = lambda n: pl.BlockSpec((1, n, HEAD), lambda b, h: (b, 0, h))
    const = pl.BlockSpec((CHUNK, CHUNK), lambda b, h: (0, 0))
    in_specs = ([head(t)] * 7 + [head(tc)] * 7 + [const] * 4
                + [pl.BlockSpec((1, HEAD), lambda b, h: (0, h))])
    out_specs = [head(t)]
    out_shape = [jax.ShapeDtypeStruct((bsz, t, E_REC), BF16)]
    if ctx_out:
        out_specs.append(head(tc))
        out_shape.append(jax.ShapeDtypeStruct((bsz, tc, E_REC), BF16))
    scratch = [pltpu.VMEM((A_UNROLL, CHUNK, HEAD), F32), pltpu.VMEM((A_UNROLL, CHUNK, HEAD), F32),
               pltpu.VMEM((n_all, CHUNK, 2 * HEAD), BF16),
               pltpu.VMEM((n_all, CHUNK, HEAD), F32),
               pltpu.VMEM((n_all, HEAD, 2 * HEAD), F32),
               pltpu.VMEM((n_all, 1, 2 * HEAD), F32),
               pltpu.VMEM((n_all, HEAD, 2 * HEAD), BF16)]
    out = pl.pallas_call(
        functools.partial(_rec_kernel, n_ctx, n_lat, ctx_out),
        grid=(bsz, REC_HEADS),
        in_specs=in_specs,
        out_specs=out_specs,
        out_shape=out_shape,
        scratch_shapes=scratch,
        compiler_params=_params("parallel", "parallel"),
        name="recurrence",
    )(*lat, *ctx, *consts, rec_g.reshape(1, E_REC))
    return out if ctx_out else (out[0], None)


def _mix_kernel(has_pos, last, d_model, *refs):
    refs = list(refs)
    a_ref, fb_ref, oc_ref, sg_ref, x_ref = refs[:5]
    refs = refs[5:]
    pos_ref = refs.pop(0) if has_pos else None
    gate_ref, wpa_ref, wpb_ref, wpc_ref, wo_ref = refs[:5]
    refs = refs[5:]
    fg_ref = refs.pop(0) if last else None
    (o_ref,) = refs

    ya = _dot(a_ref[0], wpa_ref[...])
    yb = _dot(fb_ref[0], wpb_ref[...])
    yc = _dot(oc_ref[0], wpc_ref[...])
    d = d_model
    merge = lambda j: _sigmoid(sg_ref[0, :, j * d:(j + 1) * d].astype(F32))
    y = merge(0) * ya + merge(1) * yb + merge(2) * yc
    z = _dot(y.astype(BF16), wo_ref[...])
    x = x_ref[0]
    if has_pos:
        x = x + pos_ref[...]
    xn = x + gate_ref[0] * z
    if last:
        xn = xn * lax.rsqrt(jnp.mean(xn * xn, axis=-1, keepdims=True) + EPS) * fg_ref[...]
    o_ref[0] = xn


def _mix(a, fb, oc, sg, x, pos, gate, wpa, wpb, wpc, wo, final_g, tm):
    bsz, t, d = x.shape
    has_pos = pos is not None
    last = final_g is not None
    tok = lambda w: pl.BlockSpec((1, tm, w), lambda b, i: (b, i, 0))
    full = lambda arr: pl.BlockSpec(arr.shape, lambda b, i: (0,) * arr.ndim)
    in_specs = [tok(E_CONV), tok(E_FOURIER), tok(E_REC), tok(N_BRANCH * d), tok(d)]
    args = [a, fb, oc, sg, x]
    if has_pos:
        in_specs.append(pl.BlockSpec((tm, d), lambda b, i: (i, 0)))
        args.append(pos)
    params = [wpa, wpb, wpc, wo]
    in_specs += [pl.BlockSpec((1, 1, d), lambda b, i: (b, 0, 0))] + [full(p) for p in params]
    args += [gate] + params
    if last:
        in_specs.append(full(final_g))
        args.append(final_g)
    return pl.pallas_call(
        functools.partial(_mix_kernel, has_pos, last, d),
        grid=(bsz, t // tm),
        in_specs=in_specs,
        out_specs=tok(d),
        out_shape=jax.ShapeDtypeStruct((bsz, t, d), F32),
        compiler_params=_params("parallel", "parallel"),
        name="mix",
    )(*args)


def _sincos_2d(rows, cols, d):
    quarter = d // 4
    omega = 1.0 / (10000.0 ** (jnp.arange(quarter, dtype=F32) / quarter))
    er = jnp.arange(rows, dtype=F32)[:, None] * omega
    ec = jnp.arange(cols, dtype=F32)[:, None] * omega
    emb_r = jnp.concatenate([jnp.sin(er), jnp.cos(er)], axis=-1)
    emb_c = jnp.concatenate([jnp.sin(ec), jnp.cos(ec)], axis=-1)
    emb = jnp.concatenate([jnp.broadcast_to(emb_r[:, None, :], (rows, cols, d // 2)),
                           jnp.broadcast_to(emb_c[None, :, :], (rows, cols, d // 2))], axis=-1)
    return emb.reshape(rows * cols, d)


def _tile(n, want):
    return want if n % want == 0 else n


def kernel(x, c, ctx, c_ctx, w_ada, b_ada, norm_g, w_in, conv_w, conv_b, conv_ln_g, conv_ln_b,
           rec_lb, rec_norm_g, w_pa, w_pb, w_pc, w_out, final_g):
    bsz, t, d = x.shape
    tc = ctx.shape[1]
    depth = w_in.shape[0]
    assert t % CHUNK == 0 and tc % CHUNK == 0 and t % GRID_W == 0

    pos = _sincos_2d(t // GRID_W, GRID_W, d)
    lb_soft = jax.nn.softmax(rec_lb.astype(F32), axis=1)
    lbs = jnp.cumsum(lb_soft, axis=1) - lb_soft[:, :1]

    rows = -(-(bsz + 1) // SUBLANES) * SUBLANES
    cc = jnp.zeros((rows, d), F32).at[:bsz].set(c).at[bsz].set(c_ctx)
    mod = _ada(cc, w_ada, b_ada)

    tm_x, tm_c = _tile(t, 256), _tile(tc, 256)
    for l in range(depth):
        last = l == depth - 1
        shift, scale, gate = mod[l, :, :d], mod[l, :, d:2 * d], mod[l, :, 2 * d:]
        mult = norm_g[l][None, :] * (1.0 + scale)
        mult_x, shift_x, gate_x = (a[:bsz, None, :] for a in (mult, shift, gate))
        mult_c, shift_c, gate_c = (jnp.broadcast_to(a[bsz][None, None, :], (bsz, 1, d))
                                   for a in (mult, shift, gate))
        w_l = w_in[l].astype(BF16)
        omlb = 1.0 - lbs[:, l, :]
        conv = (jnp.broadcast_to(conv_w[l].reshape(CONV_WIDTH, 1, E_CONV), (CONV_WIDTH, SUBLANES, E_CONV)),
                conv_b[l][None], conv_ln_g[l][None], conv_ln_b[l][None])

        px = _inproj(x, pos if l == 0 else None, mult_x, shift_x, w_l, omlb, conv, tm_x)
        pc = _inproj(ctx, None, mult_c, shift_c, w_l, omlb, conv, tm_c)

        ox, oc = _recurrence(px[3:10], pc[3:10], rec_norm_g[l], ctx_out=not last)

        mixw = (w_pa[l].astype(BF16), w_pb[l].astype(BF16), w_pc[l].astype(BF16), w_out[l].astype(BF16))
        fbx = _fourier(px[1], px[2], _tile(t, 1024))
        x = _mix(px[0], fbx, ox, px[10], x, pos if l == 0 else None, gate_x, *mixw,
                 final_g[None] if last else None, _tile(t, 512))
        if not last:
            fbc = _fourier(pc[1], pc[2], _tile(tc, 512))
            ctx = _mix(pc[0], fbc, oc, pc[10], ctx, None, gate_c, *mixw, None, tm_c)
    return x
```

```python
import functools

import numpy as np
import jax
import jax.numpy as jnp
from jax import lax
from jax.experimental import pallas as pl
from jax.experimental.pallas import tpu as pltpu

F32 = jnp.float32
BF16 = jnp.bfloat16

E_CONV = 768
CONV_WIDTH = 31
E_FOURIER = 512
FOURIER_GROUPS = 4
E_REC = 768
REC_HEADS = 6
HEAD = E_REC // REC_HEADS
N_BRANCH = 3
EPS = 1e-6
K_MAX = 1.0 - 1e-6
LOG2_E = 1.4426950408889634
GRID_W = 64

CHUNK = 128
BASE = 4
SUBLANES = 8
HALO = 16
CONV_ROWS = 32
VMEM_LIMIT = 56 * 1024 * 1024


def _sigmoid(x):
    return 0.5 * jnp.tanh(0.5 * x) + 0.5


def _silu(x):
    return x * _sigmoid(x)


def _dot(a, b):
    return jnp.dot(a, b, preferred_element_type=F32)


def _dot_nt(a, b):
    return lax.dot_general(a, b, (((1,), (1,)), ((), ())), preferred_element_type=F32)


def _dot_tn(a, b):
    return lax.dot_general(a, b, (((0,), (0,)), ((), ())), preferred_element_type=F32)


def _params(*sem):
    return pltpu.CompilerParams(dimension_semantics=sem, vmem_limit_bytes=VMEM_LIMIT)


def _ada_kernel(cc_ref, w_ref, b_ref, o_ref):
    s = _silu(cc_ref[...])
    o_ref[0] = jnp.dot(s, w_ref[0], preferred_element_type=F32,
                       precision=lax.Precision.HIGHEST) + b_ref[0]


def _ada(cc, w_ada, b_ada):
    depth, d, d3 = w_ada.shape
    rows = cc.shape[0]
    return pl.pallas_call(
        _ada_kernel,
        grid=(depth, d3 // d),
        in_specs=[pl.BlockSpec((rows, d), lambda l, j: (0, 0)),
                  pl.BlockSpec((1, d, d), lambda l, j: (l, 0, j)),
                  pl.BlockSpec((1, 1, d), lambda l, j: (l, 0, j))],
        out_specs=pl.BlockSpec((1, rows, d), lambda l, j: (l, 0, j)),
        out_shape=jax.ShapeDtypeStruct((depth, rows, d3), F32),
        compiler_params=_params("arbitrary", "arbitrary"),
        name="ada",
    )(cc, w_ada, b_ada.reshape(depth, 1, d3))


_SEC = {}
_off = 0
for _name, _size in (("a_val", E_CONV), ("a_gate", E_CONV), ("a_z", E_CONV),
                     ("b_u", E_FOURIER), ("b_z", E_FOURIER),
                     ("q", E_REC), ("f_fwd", E_REC), ("f_bwd", E_REC), ("i", E_REC), ("c_z", E_REC)):
    _SEC[_name] = (_off, _off + _size)
    _off += _size
GATES_OFF = _off


def _inproj_kernel(has_pos, d_model, tm, n_tiles, *refs):
    refs = list(refs)
    x_ref = refs.pop(0)
    pos_ref = refs.pop(0) if has_pos else None
    (mult_ref, shift_ref, w_ref, omlb_ref, cw_ref, cb_ref, lg_ref, lb_ref,
     a_ref, bu_ref, sbz_ref, q_ref, kf_ref, kb_ref, lf_ref, lbw_ref, v_ref, scz_ref, sg_ref,
     win_scr, ush_scr, saz_scr) = refs

    g = pl.program_id(0)

    @pl.when(g == 0)
    def _():
        win_scr[...] = jnp.zeros_like(win_scr)
        saz_scr[...] = jnp.zeros_like(saz_scr)

    x = x_ref[0]
    if has_pos:
        x = x + pos_ref[...]
    ms = jnp.mean(x * x, axis=-1, keepdims=True)
    h = (x * lax.rsqrt(ms + EPS) * mult_ref[0] + shift_ref[0]).astype(BF16)

    def proj(first, last):
        lo, hi = _SEC[first][0], _SEC[last][1]
        r = _dot(h, w_ref[:, lo:hi])
        return lambda name: r[:, _SEC[name][0] - lo:_SEC[name][1] - lo]

    starts = (g % n_tiles) == 0
    conv_p = proj("a_val", "a_z")
    u_new = conv_p("a_val") * _sigmoid(conv_p("a_gate"))
    saz_scr[g % 2] = _silu(conv_p("a_z")).astype(BF16)
    win_scr[HALO + tm:, :] = jnp.where(starts, 0.0, u_new[0:HALO])
    span = tm + 2 * HALO - SUBLANES
    for r in range(SUBLANES):
        ush_scr[r, 0:span, :] = win_scr[r:r + span, :]
    win_scr[0:HALO, :] = jnp.where(starts, 0.0, win_scr[tm:tm + HALO, :])
    win_scr[HALO:HALO + tm, :] = u_new

    def conv_rows(blk):
        base = blk * CONV_ROWS
        acc = jnp.broadcast_to(cb_ref[...], (CONV_ROWS, E_CONV))
        for j in range(CONV_WIDTH):
            off = j + HALO - CONV_WIDTH // 2
            lo = base + (off // SUBLANES) * SUBLANES
            tap = jnp.concatenate([cw_ref[j]] * (CONV_ROWS // SUBLANES), axis=0)
            acc = acc + tap * ush_scr[off % SUBLANES, lo:lo + CONV_ROWS, :]
        mu = jnp.mean(acc, axis=-1, keepdims=True)
        xc = acc - mu
        var = jnp.mean(xc * xc, axis=-1, keepdims=True)
        y = xc * lax.rsqrt(var + EPS) * lg_ref[...] + lb_ref[...]
        gate_prev = saz_scr[(g + 1) % 2, base:base + CONV_ROWS, :]
        a_ref[0, base:base + CONV_ROWS, :] = (_silu(y) * gate_prev.astype(F32)).astype(BF16)

    def sec_fourier():
        p = proj("b_u", "b_z")
        bu_ref[0] = p("b_u").astype(BF16)
        sbz_ref[0] = _silu(p("b_z")).astype(BF16)

    def sec_rec():
        p = proj("q", "c_z")
        q_ref[0] = p("q").astype(BF16)
        for name, k_ref, l_ref, row in (("f_fwd", kf_ref, lf_ref, 0), ("f_bwd", kb_ref, lbw_ref, 1)):
            k = jnp.minimum(omlb_ref[row:row + 1, :] * _sigmoid(-p(name)), K_MAX)
            k_ref[0] = k.astype(BF16)
            l_ref[0] = jnp.log(1.0 - k)
        v_ref[0] = p("i").astype(BF16)
        scz_ref[0] = _silu(p("c_z")).astype(BF16)

    def sec_gates():
        sg_ref[0] = _dot(h, w_ref[:, GATES_OFF:GATES_OFF + N_BRANCH * d_model]).astype(BF16)

    n_blk = tm // CONV_ROWS
    cuts = (n_blk // 4, (3 * n_blk) // 4, n_blk)
    for sec, lo, hi in ((sec_fourier, 0, cuts[0]), (sec_rec, cuts[0], cuts[1]), (sec_gates, cuts[1], cuts[2])):
        sec()
        for blk in range(lo, hi):
            conv_rows(blk)


def _inproj(x, pos, mult, shift, w_bf16, omlb, conv, tm):
    bsz, t, d = x.shape
    d_in = w_bf16.shape[1]
    has_pos = pos is not None
    n_tiles = t // tm
    n_all = bsz * n_tiles
    cur = lambda g: jnp.minimum(g, n_all - 1)
    prev = lambda g: jnp.maximum(g - 1, 0)
    tok = lambda w, at=cur: pl.BlockSpec((1, tm, w), lambda g: (at(g) // n_tiles, at(g) % n_tiles, 0))
    full = lambda arr: pl.BlockSpec(arr.shape, lambda g: (0,) * arr.ndim)
    in_specs = [tok(d)]
    args = [x]
    if has_pos:
        in_specs.append(pl.BlockSpec((tm, d), lambda g: (cur(g) % n_tiles, 0)))
        args.append(pos)
    in_specs += [pl.BlockSpec((1, 1, d), lambda g: (cur(g) // n_tiles, 0, 0)),
                 pl.BlockSpec((1, 1, d), lambda g: (cur(g) // n_tiles, 0, 0)),
                 pl.BlockSpec((d, d_in), lambda g: (0, 0), pipeline_mode=pl.Buffered(1)),
                 full(omlb)] + [full(c) for c in conv]
    args += [mult, shift, w_bf16, omlb, *conv]
    widths = [(E_FOURIER, BF16), (E_FOURIER, BF16),
              (E_REC, BF16), (E_REC, BF16), (E_REC, BF16), (E_REC, F32), (E_REC, F32),
              (E_REC, BF16), (E_REC, BF16), (N_BRANCH * d, BF16)]
    return pl.pallas_call(
        functools.partial(_inproj_kernel, has_pos, d, tm, n_tiles),
        grid=(n_all + 1,),
        in_specs=in_specs,
        out_specs=[tok(E_CONV, prev)] + [tok(w) for w, _ in widths],
        out_shape=[jax.ShapeDtypeStruct((bsz, t, w), dt) for w, dt in [(E_CONV, BF16)] + widths],
        scratch_shapes=[pltpu.VMEM((tm + 2 * HALO, E_CONV), F32),
                        pltpu.VMEM((SUBLANES, tm + 2 * HALO, E_CONV), F32),
                        pltpu.VMEM((2, tm, E_CONV), BF16)],
        compiler_params=_params("arbitrary"),
        name="inproj",
    )(*args)


def _inproj_states_kernel(x_ref, mult_ref, shift_ref, w_ref, omlb_ref, kf_ref, kb_ref, lf_ref, lbw_ref, v_ref):
    x = x_ref[0]
    ms = jnp.mean(x * x, axis=-1, keepdims=True)
    h = (x * lax.rsqrt(ms + EPS) * mult_ref[0] + shift_ref[0]).astype(BF16)
    r = _dot(h, w_ref[...])
    for row, k_ref, l_ref in ((0, kf_ref, lf_ref), (1, kb_ref, lbw_ref)):
        k = jnp.minimum(omlb_ref[row:row + 1, :] * _sigmoid(-r[:, row * E_REC:(row + 1) * E_REC]), K_MAX)
        k_ref[0] = k.astype(BF16)
        l_ref[0] = jnp.log(1.0 - k)
    v_ref[0] = r[:, 2 * E_REC:].astype(BF16)


def _inproj_states(x, mult, shift, w_bf16, omlb, tm):
    bsz, t, d = x.shape
    lo, hi = _SEC["f_fwd"][0], _SEC["i"][1]
    assert (_SEC["f_bwd"][0], _SEC["i"][0]) == (lo + E_REC, lo + 2 * E_REC)
    w_bf16 = w_bf16[:, lo:hi]
    tok = lambda w: pl.BlockSpec((1, tm, w), lambda b, i: (b, i, 0))
    vec = pl.BlockSpec((1, 1, d), lambda b, i: (b, 0, 0))
    widths = [(E_REC, BF16), (E_REC, BF16), (E_REC, F32), (E_REC, F32), (E_REC, BF16)]
    return pl.pallas_call(
        _inproj_states_kernel,
        grid=(bsz, t // tm),
        in_specs=[tok(d), vec, vec,
                  pl.BlockSpec(w_bf16.shape, lambda b, i: (0, 0)),
                  pl.BlockSpec(omlb.shape, lambda b, i: (0, 0))],
        out_specs=[tok(w) for w, _ in widths],
        out_shape=[jax.ShapeDtypeStruct((bsz, t, w), dt) for w, dt in widths],
        compiler_params=_params("parallel", "parallel"),
        name="inproj_states",
    )(x, mult, shift, w_bf16, omlb)


def _fourier_kernel(t, scale, u_ref, gate_ref, cs_ref, cc_ref, o_ref, ab_scr):
    gw = E_FOURIER // FOURIER_GROUPS

    @pl.when(pl.program_id(1) == 0)
    def _():
        for g in range(FOURIER_GROUPS):
            r = _dot(u_ref[0, :, g * gw:(g + 1) * gw], cc_ref[...])
            ab_scr[0:t, g * gw:(g + 1) * gw] = r[:, :gw].astype(BF16)
            ab_scr[t:2 * t, g * gw:(g + 1) * gw] = r[:, gw:].astype(BF16)

    f = _dot(cs_ref[...], ab_scr[...])
    o_ref[0] = (f * scale * gate_ref[0].astype(F32)).astype(BF16)


def _dft_consts(t):
    gw = E_FOURIER // FOURIER_GROUPS
    jk = np.outer(np.arange(t), np.arange(t)) % t
    ang = 2.0 * np.pi * jk / t
    cs = np.concatenate([np.cos(ang), -np.sin(ang)], axis=1)
    jc = np.outer(np.arange(gw), np.arange(gw)) % gw
    angc = 2.0 * np.pi * jc / gw
    cc = np.concatenate([np.cos(angc), np.sin(angc)], axis=1)
    scale = 1.0 / np.sqrt(float(t) * gw)
    return jnp.asarray(cs, dtype=F32).astype(BF16), jnp.asarray(cc, dtype=F32).astype(BF16), float(scale)


def _fourier(bu, sbz, tr):
    bsz, t, e = bu.shape
    gw = e // FOURIER_GROUPS
    cs, cc, scale = _dft_consts(t)
    return pl.pallas_call(
        functools.partial(_fourier_kernel, t, scale),
        grid=(bsz, t // tr),
        in_specs=[pl.BlockSpec((1, t, e), lambda b, i: (b, 0, 0)),
                  pl.BlockSpec((1, tr, e), lambda b, i: (b, i, 0)),
                  pl.BlockSpec((tr, 2 * t), lambda b, i: (i, 0)),
                  pl.BlockSpec((gw, 2 * gw), lambda b, i: (0, 0))],
        out_specs=pl.BlockSpec((1, tr, e), lambda b, i: (b, i, 0)),
        out_shape=jax.ShapeDtypeStruct((bsz, t, e), BF16),
        scratch_shapes=[pltpu.VMEM((2 * t, e), BF16)],
        compiler_params=_params("parallel", "arbitrary"),
        name="fourier",
    )(bu, sbz, cs, cc)


def _level_ids():
    t = np.arange(CHUNK)[:, None]
    s = np.arange(CHUNK)[None, :]
    lv = np.full((CHUNK, CHUNK), -1, np.int32)
    size, level = CHUNK, int(np.log2(CHUNK // BASE))
    while size >= BASE:
        lv = np.where((t // size == s // size) & (s <= t), level, lv)
        size //= 2
        level -= 1
    return lv.astype(np.int32)


N_LEVELS = int(np.log2(CHUNK // BASE)) + 1
A_UNROLL = 8
STATIC_ITERS = 2
C_UNROLL = 8


def _chunk_rows(c):
    if isinstance(c, int):
        return pl.ds(c * CHUNK, CHUNK)
    return pl.ds(pl.multiple_of(c * CHUNK, CHUNK), CHUNK)


def _row(ref, r):
    return jnp.broadcast_to(ref[pl.ds(r, 1), :], (SUBLANES, HEAD))


def _level_factors(level, bc, bc_ref, q, k, fwd):
    groups = CHUNK // SUBLANES
    sub = lax.broadcasted_iota(jnp.int32, (CHUNK, HEAD), 0) % SUBLANES
    if level == 0:
        lo_row, hi_row = (0, BASE) if fwd else (BASE - 1, SUBLANES - 1)
        ref = jnp.concatenate(
            [jnp.where(sub[:SUBLANES] < BASE, _row(bc_ref, g * SUBLANES + lo_row),
                       _row(bc_ref, g * SUBLANES + hi_row)) for g in range(groups)], axis=0)
        e = bc - ref
        return (q * jnp.exp2(e)).astype(BF16), (k * jnp.exp2(-e)).astype(BF16), [slice(0, CHUNK)]
    half = BASE << (level - 1)
    if half < SUBLANES:
        ref_row = half - 1 if fwd else half
        ref = jnp.concatenate([_row(bc_ref, g * SUBLANES + ref_row) for g in range(groups)], axis=0)
        q_rows = (sub >= half) if fwd else (sub < half)
        ex = jnp.exp2(jnp.where(q_rows, bc - ref, ref - bc))
        zero = jnp.zeros_like(ex)
        return (jnp.where(q_rows, q * ex, zero).astype(BF16),
                jnp.where(q_rows, zero, k * ex).astype(BF16), [slice(0, CHUNK)])
    qs, ks, q_rows = [], [], []
    zero = jnp.zeros((half, HEAD), F32)
    for blk in range(CHUNK // (2 * half)):
        a = blk * 2 * half
        lo, hi = slice(a, a + half), slice(a + half, a + 2 * half)
        ref = jnp.broadcast_to(bc_ref[pl.ds(a + half - 1 if fwd else a + half, 1), :], (half, HEAD))
        if fwd:
            qs.append(q[hi] * jnp.exp2(bc[hi] - ref))
            ks += [k[lo] * jnp.exp2(ref - bc[lo]), zero]
            q_rows.append(hi)
        else:
            qs.append(q[lo] * jnp.exp2(bc[lo] - ref))
            ks += [zero, k[hi] * jnp.exp2(ref - bc[hi])]
            q_rows.append(lo)
    return jnp.concatenate(qs, axis=0).astype(BF16), jnp.concatenate(ks, axis=0).astype(BF16), q_rows


def _split2(g):
    hi = g.astype(BF16)
    lo = (g - hi.astype(F32)).astype(BF16)
    return jnp.concatenate([hi, lo], axis=1)


def _rec_kernel(n_ctx, n_lat, ctx_out, *refs):
    (qx, kfx, kbx, lfx, lbx, vx, zx, qc, kfc, kbc, lfc, lbc, vc, zc,
     tril_ref, triu_ref, lvf_ref, lvb_ref, g_ref) = refs[:19]
    if ctx_out:
        ox_ref, oc_ref = refs[19:21]
        scr = refs[21:]
    else:
        ox_ref, oc_ref = refs[19], None
        scr = refs[20:]
    bcf_scr, bcb_scr, qfb_scr, oin_scr, dst_scr, a_scr, st_scr = scr
    kw = HEAD

    def chunks_a(q_ref, kf_ref, kb_ref, lf_ref, lb_ref, v_ref, chunks, slot0, outputs):
        dirs = ((True, kf_ref, lf_ref, tril_ref, lvf_ref, bcf_scr),
                (False, kb_ref, lb_ref, triu_ref, lvb_ref, bcb_scr))
        rows = [_chunk_rows(c) for c in chunks]
        units = [(u, d) for u in range(len(chunks)) for d in range(2)]
        q = [q_ref[0, r, :].astype(F32) for r in rows] if outputs else None
        v = [v_ref[0, r, :] for r in rows]
        k = {(u, d): dirs[d][1][0, rows[u], :].astype(F32) for u, d in units}
        cs = {(u, d): _dot(dirs[d][3][...], _split2(dirs[d][2][0, rows[u], :])) for u, d in units}
        bc, bref = {}, {}
        for u, d in units:
            bc[u, d] = (cs[u, d][:, :kw] + cs[u, d][:, kw:]) * LOG2_E
            bref[u, d] = dirs[d][5].at[u]
            bref[u, d][...] = bc[u, d]
        qi, kd, dec = {}, {}, {}
        for u, d in units:
            b_edge = bref[u, d][pl.ds(CHUNK - 1 if d == 0 else 0, 1), :]
            if outputs:
                qi[u, d] = (q[u] * jnp.exp2(bc[u, d])).astype(BF16)
            kd[u, d] = (k[u, d] * jnp.exp2(b_edge - bc[u, d])).astype(BF16)
            dec[u, d] = jnp.exp2(b_edge)
        groups = CHUNK // SUBLANES
        sc = {ud: [jnp.zeros((SUBLANES, CHUNK), F32) for _ in range(groups)] for ud in units}
        for level in range(N_LEVELS if outputs else 0):
            for u, d in units:
                qt, kt, q_rows = _level_factors(level, bc[u, d], bref[u, d], q[u], k[u, d], d == 0)
                p = _dot_nt(qt, kt)
                at = 0
                for sl in q_rows:
                    for g in range(sl.start // SUBLANES, sl.stop // SUBLANES):
                        own = dirs[d][4][g * SUBLANES:(g + 1) * SUBLANES, :] == level
                        sc[u, d][g] = jnp.where(own, p[at:at + SUBLANES], sc[u, d][g])
                        at += SUBLANES
        for u in range(len(chunks)):
            slot = slot0 + chunks[u]
            if outputs:
                scores = (jnp.concatenate(sc[u, 0], axis=0) + jnp.concatenate(sc[u, 1], axis=0)).astype(BF16)
                oin_scr[slot] = _dot(scores, v[u])
                qfb_scr[slot] = jnp.concatenate([qi[u, 0], qi[u, 1]], axis=1)
            dst_scr[slot] = _dot_tn(v[u], jnp.concatenate([kd[u, 0], kd[u, 1]], axis=1))
            a_scr[slot] = jnp.concatenate([dec[u, 0], dec[u, 1]], axis=1)

    def chunk_c(z_ref, o_ref, c, slot):
        rows = _chunk_rows(c)
        o = oin_scr[slot] + _dot_nt(qfb_scr[slot], st_scr[slot])
        o = o * lax.rsqrt(jnp.mean(o * o, axis=-1, keepdims=True) + EPS)
        o_ref[0, rows, :] = (o * g_ref[...] * z_ref[0, rows, :].astype(F32)).astype(BF16)

    def for_chunks(n, per_iter, fn):
        per_iter = max(u for u in range(1, per_iter + 1) if n % u == 0)

        def body(i, carry):
            fn([i * per_iter + u for u in range(per_iter)])
            return carry

        if n // per_iter <= STATIC_ITERS:
            for i in range(n // per_iter):
                body(i, 0)
        else:
            lax.fori_loop(0, n // per_iter, body, 0)

    for_chunks(n_ctx, A_UNROLL, lambda cs: chunks_a(qc, kfc, kbc, lfc, lbc, vc, cs, 0, ctx_out))
    for_chunks(n_lat, A_UNROLL, lambda cs: chunks_a(qx, kfx, kbx, lfx, lbx, vx, cs, n_ctx, True))

    n_all = n_ctx + n_lat
    fwd_order = list(range(n_all))
    bwd_order = list(range(n_ctx - 1, -1, -1)) + list(range(n_all - 1, n_ctx - 1, -1))
    for order, cols in ((fwd_order, slice(0, kw)), (bwd_order, slice(kw, 2 * kw))):
        s = jnp.zeros((kw, kw), F32)
        for slot in order:
            st_scr[slot, :, cols] = s.astype(BF16)
            s = a_scr[slot, :, cols] * s + dst_scr[slot, :, cols]

    for_chunks(n_lat, C_UNROLL, lambda cs: [chunk_c(zx, ox_ref, c, c + n_ctx) for c in cs])
    if ctx_out:
        for_chunks(n_ctx, C_UNROLL, lambda cs: [chunk_c(zc, oc_ref, c, c) for c in cs])


def _recurrence(lat, ctx, rec_g, ctx_out):
    bsz, t, _ = lat[0].shape
    tc = ctx[0].shape[1]
    n_lat, n_ctx = t // CHUNK, tc // CHUNK
    n_all = n_lat + n_ctx
    lvf = _level_ids()
    tri = np.tril(np.ones((CHUNK, CHUNK), np.float32))
    consts = [jnp.asarray(tri, dtype=BF16), jnp.asarray(tri.T, dtype=BF16),
              jnp.asarray(lvf), jnp.asarray(lvf.T)]
    head = lambda n: pl.BlockSpec((1, n, HEAD), lambda b, h: (b, 0, h))
    const = pl.BlockSpec((CHUNK, CHUNK), lambda b, h: (0, 0))
    in_specs = ([head(t)] * 7 + [head(tc)] * 7 + [const] * 4
                + [pl.BlockSpec((1, HEAD), lambda b, h: (0, h))])
    out_specs = [head(t)]
    out_shape = [jax.ShapeDtypeStruct((bsz, t, E_REC), BF16)]
    if ctx_out:
        out_specs.append(head(tc))
        out_shape.append(jax.ShapeDtypeStruct((bsz, tc, E_REC), BF16))
    scratch = [pltpu.VMEM((A_UNROLL, CHUNK, HEAD), F32), pltpu.VMEM((A_UNROLL, CHUNK, HEAD), F32),
               pltpu.VMEM((n_all, CHUNK, 2 * HEAD), BF16),
               pltpu.VMEM((n_all, CHUNK, HEAD), F32),
               pltpu.VMEM((n_all, HEAD, 2 * HEAD), F32),
               pltpu.VMEM((n_all, 1, 2 * HEAD), F32),
               pltpu.VMEM((n_all, HEAD, 2 * HEAD), BF16)]
    out = pl.pallas_call(
        functools.partial(_rec_kernel, n_ctx, n_lat, ctx_out),
        grid=(bsz, REC_HEADS),
        in_specs=in_specs,
        out_specs=out_specs,
        out_shape=out_shape,
        scratch_shapes=scratch,
        compiler_params=_params("parallel", "parallel"),
        name="recurrence",
    )(*lat, *ctx, *consts, rec_g.reshape(1, E_REC))
    return out if ctx_out else (out[0], None)


def _mix_kernel(has_pos, last, d_model, *refs):
    refs = list(refs)
    a_ref, fb_ref, oc_ref, sg_ref, x_ref = refs[:5]
    refs = refs[5:]
    pos_ref = refs.pop(0) if has_pos else None
    gate_ref, wpa_ref, wpb_ref, wpc_ref, wo_ref = refs[:5]
    refs = refs[5:]
    fg_ref = refs.pop(0) if last else None
    (o_ref,) = refs

    ya = _dot(a_ref[0], wpa_ref[...])
    yb = _dot(fb_ref[0], wpb_ref[...])
    yc = _dot(oc_ref[0], wpc_ref[...])
    d = d_model
    merge = lambda j: _sigmoid(sg_ref[0, :, j * d:(j + 1) * d].astype(F32))
    y = merge(0) * ya + merge(1) * yb + merge(2) * yc
    z = _dot(y.astype(BF16), wo_ref[...])
    x = x_ref[0]
    if has_pos:
        x = x + pos_ref[...]
    xn = x + gate_ref[0] * z
    if last:
        xn = xn * lax.rsqrt(jnp.mean(xn * xn, axis=-1, keepdims=True) + EPS) * fg_ref[...]
    o_ref[0] = xn


def _mix(a, fb, oc, sg, x, pos, gate, wpa, wpb, wpc, wo, final_g, tm):
    bsz, t, d = x.shape
    has_pos = pos is not None
    last = final_g is not None
    tok = lambda w: pl.BlockSpec((1, tm, w), lambda b, i: (b, i, 0))
    full = lambda arr: pl.BlockSpec(arr.shape, lambda b, i: (0,) * arr.ndim)
    in_specs = [tok(E_CONV), tok(E_FOURIER), tok(E_REC), tok(N_BRANCH * d), tok(d)]
    args = [a, fb, oc, sg, x]
    if has_pos:
        in_specs.append(pl.BlockSpec((tm, d), lambda b, i: (i, 0)))
        args.append(pos)
    params = [wpa, wpb, wpc, wo]
    in_specs += [pl.BlockSpec((1, 1, d), lambda b, i: (b, 0, 0))] + [full(p) for p in params]
    args += [gate] + params
    if last:
        in_specs.append(full(final_g))
        args.append(final_g)
    return pl.pallas_call(
        functools.partial(_mix_kernel, has_pos, last, d),
        grid=(bsz, t // tm),
        in_specs=in_specs,
        out_specs=tok(d),
        out_shape=jax.ShapeDtypeStruct((bsz, t, d), F32),
        compiler_params=_params("parallel", "parallel"),
        name="mix",
    )(*args)


def _sincos_2d(rows, cols, d):
    quarter = d // 4
    omega = 1.0 / (10000.0 ** (jnp.arange(quarter, dtype=F32) / quarter))
    er = jnp.arange(rows, dtype=F32)[:, None] * omega
    ec = jnp.arange(cols, dtype=F32)[:, None] * omega
    emb_r = jnp.concatenate([jnp.sin(er), jnp.cos(er)], axis=-1)
    emb_c = jnp.concatenate([jnp.sin(ec), jnp.cos(ec)], axis=-1)
    emb = jnp.concatenate([jnp.broadcast_to(emb_r[:, None, :], (rows, cols, d // 2)),
                           jnp.broadcast_to(emb_c[None, :, :], (rows, cols, d // 2))], axis=-1)
    return emb.reshape(rows * cols, d)


def _tile(n, want):
    return want if n % want == 0 else n


def kernel(x, c, ctx, c_ctx, w_ada, b_ada, norm_g, w_in, conv_w, conv_b, conv_ln_g, conv_ln_b,
           rec_lb, rec_norm_g, w_pa, w_pb, w_pc, w_out, final_g):
    bsz, t, d = x.shape
    tc = ctx.shape[1]
    depth = w_in.shape[0]
    assert t % CHUNK == 0 and tc % CHUNK == 0 and t % GRID_W == 0

    pos = _sincos_2d(t // GRID_W, GRID_W, d)
    lb_soft = jax.nn.softmax(rec_lb.astype(F32), axis=1)
    lbs = jnp.cumsum(lb_soft, axis=1) - lb_soft[:, :1]

    rows = -(-(bsz + 1) // SUBLANES) * SUBLANES
    cc = jnp.zeros((rows, d), F32).at[:bsz].set(c).at[bsz].set(c_ctx)
    mod = _ada(cc, w_ada, b_ada)

    tm_x, tm_c = _tile(t, 256), _tile(tc, 256)
    for l in range(depth):
        last = l == depth - 1
        shift, scale, gate = mod[l, :, :d], mod[l, :, d:2 * d], mod[l, :, 2 * d:]
        mult = norm_g[l][None, :] * (1.0 + scale)
        mult_x, shift_x, gate_x = (a[:bsz, None, :] for a in (mult, shift, gate))
        mult_c, shift_c, gate_c = (jnp.broadcast_to(a[bsz][None, None, :], (bsz, 1, d))
                                   for a in (mult, shift, gate))
        w_l = w_in[l].astype(BF16)
        omlb = 1.0 - lbs[:, l, :]
        conv = (jnp.broadcast_to(conv_w[l].reshape(CONV_WIDTH, 1, E_CONV), (CONV_WIDTH, SUBLANES, E_CONV)),
                conv_b[l][None], conv_ln_g[l][None], conv_ln_b[l][None])

        px = _inproj(x, pos if l == 0 else None, mult_x, shift_x, w_l, omlb, conv, tm_x)
        if last:
            kf_c, kb_c, lf_c, lb_c, v_c = _inproj_states(ctx, mult_c, shift_c, w_l, omlb, tm_c)
            rec_c = (kf_c, kf_c, kb_c, lf_c, lb_c, v_c, kf_c)
        else:
            pc = _inproj(ctx, None, mult_c, shift_c, w_l, omlb, conv, tm_c)
            rec_c = pc[3:10]

        ox, oc = _recurrence(px[3:10], rec_c, rec_norm_g[l], ctx_out=not last)

        mixw = (w_pa[l].astype(BF16), w_pb[l].astype(BF16), w_pc[l].astype(BF16), w_out[l].astype(BF16))
        fbx = _fourier(px[1], px[2], _tile(t, 1024))
        x = _mix(px[0], fbx, ox, px[10], x, pos if l == 0 else None, gate_x, *mixw,
                 final_g[None] if last else None, _tile(t, 512))
        if not last:
            fbc = _fourier(pc[1], pc[2], _tile(tc, 512))
            ctx = _mix(pc[0], fbc, oc, pc[10], ctx, None, gate_c, *mixw, None, tm_c)
    return x
```

```python
import functools

import numpy as np
import jax
import jax.numpy as jnp
from jax import lax
from jax.experimental import pallas as pl
from jax.experimental.pallas import tpu as pltpu

F32 = jnp.float32
BF16 = jnp.bfloat16

E_CONV = 768
CONV_WIDTH = 31
E_FOURIER = 512
FOURIER_GROUPS = 4
E_REC = 768
REC_HEADS = 6
HEAD = E_REC // REC_HEADS
N_BRANCH = 3
EPS = 1e-6
K_MAX = 1.0 - 1e-6
LOG2_E = 1.4426950408889634
GRID_W = 64

CHUNK = 128
BASE = 4
SUBLANES = 8
HALO = 16
CONV_ROWS = 32
VMEM_LIMIT = 56 * 1024 * 1024


def _sigmoid(x):
    return 0.5 * jnp.tanh(0.5 * x) + 0.5


def _silu(x):
    return x * _sigmoid(x)


def _dot(a, b):
    return jnp.dot(a, b, preferred_element_type=F32)


def _dot_nt(a, b):
    return lax.dot_general(a, b, (((1,), (1,)), ((), ())), preferred_element_type=F32)


def _dot_tn(a, b):
    return lax.dot_general(a, b, (((0,), (0,)), ((), ())), preferred_element_type=F32)


def _params(*sem):
    return pltpu.CompilerParams(dimension_semantics=sem, vmem_limit_bytes=VMEM_LIMIT)


def _ada_kernel(cc_ref, w_ref, b_ref, o_ref):
    s = _silu(cc_ref[...])
    o_ref[0] = jnp.dot(s, w_ref[0], preferred_element_type=F32,
                       precision=lax.Precision.HIGHEST) + b_ref[0]


def _ada(cc, w_ada, b_ada):
    depth, d, d3 = w_ada.shape
    rows = cc.shape[0]
    return pl.pallas_call(
        _ada_kernel,
        grid=(depth, d3 // d),
        in_specs=[pl.BlockSpec((rows, d), lambda l, j: (0, 0)),
                  pl.BlockSpec((1, d, d), lambda l, j: (l, 0, j)),
                  pl.BlockSpec((1, 1, d), lambda l, j: (l, 0, j))],
        out_specs=pl.BlockSpec((1, rows, d), lambda l, j: (l, 0, j)),
        out_shape=jax.ShapeDtypeStruct((depth, rows, d3), F32),
        compiler_params=_params("arbitrary", "arbitrary"),
        name="ada",
    )(cc, w_ada, b_ada.reshape(depth, 1, d3))


_SEC = {}
_off = 0
for _name, _size in (("a_val", E_CONV), ("a_gate", E_CONV), ("a_z", E_CONV),
                     ("b_u", E_FOURIER), ("b_z", E_FOURIER),
                     ("q", E_REC), ("f_fwd", E_REC), ("f_bwd", E_REC), ("i", E_REC), ("c_z", E_REC)):
    _SEC[_name] = (_off, _off + _size)
    _off += _size
GATES_OFF = _off


def _inproj_kernel(has_pos, d_model, tm, n_tiles, *refs):
    refs = list(refs)
    x_ref = refs.pop(0)
    pos_ref = refs.pop(0) if has_pos else None
    (mult_ref, shift_ref, w_ref, omlb_ref, cw_ref, cb_ref, lg_ref, lb_ref,
     a_ref, bu_ref, sbz_ref, q_ref, kf_ref, kb_ref, lf_ref, lbw_ref, v_ref, scz_ref, sg_ref,
     win_scr, ush_scr, saz_scr) = refs

    g = pl.program_id(0)

    @pl.when(g == 0)
    def _():
        win_scr[...] = jnp.zeros_like(win_scr)
        saz_scr[...] = jnp.zeros_like(saz_scr)

    x = x_ref[0]
    if has_pos:
        x = x + pos_ref[...]
    ms = jnp.mean(x * x, axis=-1, keepdims=True)
    h = (x * lax.rsqrt(ms + EPS) * mult_ref[0] + shift_ref[0]).astype(BF16)

    def proj(first, last):
        lo, hi = _SEC[first][0], _SEC[last][1]
        r = _dot(h, w_ref[:, lo:hi])
        return lambda name: r[:, _SEC[name][0] - lo:_SEC[name][1] - lo]

    starts = (g % n_tiles) == 0
    conv_p = proj("a_val", "a_z")
    u_new = conv_p("a_val") * _sigmoid(conv_p("a_gate"))
    saz_scr[g % 2] = _silu(conv_p("a_z")).astype(BF16)
    win_scr[HALO + tm:, :] = jnp.where(starts, 0.0, u_new[0:HALO])
    span = tm + 2 * HALO - SUBLANES
    for r in range(SUBLANES):
        ush_scr[r, 0:span, :] = win_scr[r:r + span, :]
    win_scr[0:HALO, :] = jnp.where(starts, 0.0, win_scr[tm:tm + HALO, :])
    win_scr[HALO:HALO + tm, :] = u_new

    def conv_rows(blk):
        base = blk * CONV_ROWS
        acc = jnp.broadcast_to(cb_ref[...], (CONV_ROWS, E_CONV))
        for j in range(CONV_WIDTH):
            off = j + HALO - CONV_WIDTH // 2
            lo = base + (off // SUBLANES) * SUBLANES
            tap = jnp.concatenate([cw_ref[j]] * (CONV_ROWS // SUBLANES), axis=0)
            acc = acc + tap * ush_scr[off % SUBLANES, lo:lo + CONV_ROWS, :]
        mu = jnp.mean(acc, axis=-1, keepdims=True)
        xc = acc - mu
        var = jnp.mean(xc * xc, axis=-1, keepdims=True)
        y = xc * lax.rsqrt(var + EPS) * lg_ref[...] + lb_ref[...]
        gate_prev = saz_scr[(g + 1) % 2, base:base + CONV_ROWS, :]
        a_ref[0, base:base + CONV_ROWS, :] = (_silu(y) * gate_prev.astype(F32)).astype(BF16)

    def sec_fourier():
        p = proj("b_u", "b_z")
        bu_ref[0] = p("b_u").astype(BF16)
        sbz_ref[0] = _silu(p("b_z")).astype(BF16)

    def sec_rec():
        p = proj("q", "c_z")
        q_ref[0] = p("q").astype(BF16)
        for name, k_ref, l_ref, row in (("f_fwd", kf_ref, lf_ref, 0), ("f_bwd", kb_ref, lbw_ref, 1)):
            k = jnp.minimum(omlb_ref[row:row + 1, :] * _sigmoid(-p(name)), K_MAX)
            k_ref[0] = k.astype(BF16)
            l_ref[0] = jnp.log(1.0 - k)
        v_ref[0] = p("i").astype(BF16)
        scz_ref[0] = _silu(p("c_z")).astype(BF16)

    def sec_gates():
        sg_ref[0] = _dot(h, w_ref[:, GATES_OFF:GATES_OFF + N_BRANCH * d_model]).astype(BF16)

    n_blk = tm // CONV_ROWS
    cuts = (n_blk // 4, (3 * n_blk) // 4, n_blk)
    for sec, lo, hi in ((sec_fourier, 0, cuts[0]), (sec_rec, cuts[0], cuts[1]), (sec_gates, cuts[1], cuts[2])):
        sec()
        for blk in range(lo, hi):
            conv_rows(blk)


def _inproj(x, pos, mult, shift, w_layers, layer, omlb, conv, tm):
    bsz, t, d = x.shape
    d_in = w_layers.shape[2]
    has_pos = pos is not None
    n_tiles = t // tm
    n_all = bsz * n_tiles
    cur = lambda g: jnp.minimum(g, n_all - 1)
    prev = lambda g: jnp.maximum(g - 1, 0)
    tok = lambda w, at=cur: pl.BlockSpec((1, tm, w), lambda g: (at(g) // n_tiles, at(g) % n_tiles, 0))
    full = lambda arr: pl.BlockSpec(arr.shape, lambda g: (0,) * arr.ndim)
    in_specs = [tok(d)]
    args = [x]
    if has_pos:
        in_specs.append(pl.BlockSpec((tm, d), lambda g: (cur(g) % n_tiles, 0)))
        args.append(pos)
    in_specs += [pl.BlockSpec((1, 1, d), lambda g: (cur(g) // n_tiles, 0, 0)),
                 pl.BlockSpec((1, 1, d), lambda g: (cur(g) // n_tiles, 0, 0)),
                 pl.BlockSpec((None, d, d_in), lambda g: (layer, 0, 0), pipeline_mode=pl.Buffered(1)),
                 full(omlb)] + [full(c) for c in conv]
    args += [mult, shift, w_layers, omlb, *conv]
    widths = [(E_FOURIER, BF16), (E_FOURIER, BF16),
              (E_REC, BF16), (E_REC, BF16), (E_REC, BF16), (E_REC, F32), (E_REC, F32),
              (E_REC, BF16), (E_REC, BF16), (N_BRANCH * d, BF16)]
    return pl.pallas_call(
        functools.partial(_inproj_kernel, has_pos, d, tm, n_tiles),
        grid=(n_all + 1,),
        in_specs=in_specs,
        out_specs=[tok(E_CONV, prev)] + [tok(w) for w, _ in widths],
        out_shape=[jax.ShapeDtypeStruct((bsz, t, w), dt) for w, dt in [(E_CONV, BF16)] + widths],
        scratch_shapes=[pltpu.VMEM((tm + 2 * HALO, E_CONV), F32),
                        pltpu.VMEM((SUBLANES, tm + 2 * HALO, E_CONV), F32),
                        pltpu.VMEM((2, tm, E_CONV), BF16)],
        compiler_params=_params("arbitrary"),
        name="inproj",
    )(*args)


def _inproj_states_kernel(x_ref, mult_ref, shift_ref, w_ref, omlb_ref, kf_ref, kb_ref, lf_ref, lbw_ref, v_ref):
    x = x_ref[0]
    ms = jnp.mean(x * x, axis=-1, keepdims=True)
    h = (x * lax.rsqrt(ms + EPS) * mult_ref[0] + shift_ref[0]).astype(BF16)
    r = _dot(h, w_ref[...])
    for row, k_ref, l_ref in ((0, kf_ref, lf_ref), (1, kb_ref, lbw_ref)):
        k = jnp.minimum(omlb_ref[row:row + 1, :] * _sigmoid(-r[:, row * E_REC:(row + 1) * E_REC]), K_MAX)
        k_ref[0] = k.astype(BF16)
        l_ref[0] = jnp.log(1.0 - k)
    v_ref[0] = r[:, 2 * E_REC:].astype(BF16)


def _inproj_states(x, mult, shift, w_bf16, omlb, tm):
    bsz, t, d = x.shape
    lo, hi = _SEC["f_fwd"][0], _SEC["i"][1]
    assert (_SEC["f_bwd"][0], _SEC["i"][0]) == (lo + E_REC, lo + 2 * E_REC)
    w_bf16 = w_bf16[:, lo:hi]
    tok = lambda w: pl.BlockSpec((1, tm, w), lambda b, i: (b, i, 0))
    vec = pl.BlockSpec((1, 1, d), lambda b, i: (b, 0, 0))
    widths = [(E_REC, BF16), (E_REC, BF16), (E_REC, F32), (E_REC, F32), (E_REC, BF16)]
    return pl.pallas_call(
        _inproj_states_kernel,
        grid=(bsz, t // tm),
        in_specs=[tok(d), vec, vec,
                  pl.BlockSpec(w_bf16.shape, lambda b, i: (0, 0)),
                  pl.BlockSpec(omlb.shape, lambda b, i: (0, 0))],
        out_specs=[tok(w) for w, _ in widths],
        out_shape=[jax.ShapeDtypeStruct((bsz, t, w), dt) for w, dt in widths],
        compiler_params=_params("parallel", "parallel"),
        name="inproj_states",
    )(x, mult, shift, w_bf16, omlb)


def _fourier_kernel(t, scale, u_ref, gate_ref, cs_ref, cc_ref, o_ref, ab_scr):
    gw = E_FOURIER // FOURIER_GROUPS

    @pl.when(pl.program_id(1) == 0)
    def _():
        for g in range(FOURIER_GROUPS):
            r = _dot(u_ref[0, :, g * gw:(g + 1) * gw], cc_ref[...])
            ab_scr[0:t, g * gw:(g + 1) * gw] = r[:, :gw].astype(BF16)
            ab_scr[t:2 * t, g * gw:(g + 1) * gw] = r[:, gw:].astype(BF16)

    f = _dot(cs_ref[...], ab_scr[...])
    o_ref[0] = (f * scale * gate_ref[0].astype(F32)).astype(BF16)


def _dft_consts(t):
    gw = E_FOURIER // FOURIER_GROUPS
    jk = np.outer(np.arange(t), np.arange(t)) % t
    ang = 2.0 * np.pi * jk / t
    cs = np.concatenate([np.cos(ang), -np.sin(ang)], axis=1)
    jc = np.outer(np.arange(gw), np.arange(gw)) % gw
    angc = 2.0 * np.pi * jc / gw
    cc = np.concatenate([np.cos(angc), np.sin(angc)], axis=1)
    scale = 1.0 / np.sqrt(float(t) * gw)
    return jnp.asarray(cs, dtype=F32).astype(BF16), jnp.asarray(cc, dtype=F32).astype(BF16), float(scale)


def _fourier(bu, sbz, tr):
    bsz, t, e = bu.shape
    gw = e // FOURIER_GROUPS
    cs, cc, scale = _dft_consts(t)
    return pl.pallas_call(
        functools.partial(_fourier_kernel, t, scale),
        grid=(bsz, t // tr),
        in_specs=[pl.BlockSpec((1, t, e), lambda b, i: (b, 0, 0)),
                  pl.BlockSpec((1, tr, e), lambda b, i: (b, i, 0)),
                  pl.BlockSpec((tr, 2 * t), lambda b, i: (i, 0)),
                  pl.BlockSpec((gw, 2 * gw), lambda b, i: (0, 0))],
        out_specs=pl.BlockSpec((1, tr, e), lambda b, i: (b, i, 0)),
        out_shape=jax.ShapeDtypeStruct((bsz, t, e), BF16),
        scratch_shapes=[pltpu.VMEM((2 * t, e), BF16)],
        compiler_params=_params("parallel", "arbitrary"),
        name="fourier",
    )(bu, sbz, cs, cc)


def _level_ids():
    t = np.arange(CHUNK)[:, None]
    s = np.arange(CHUNK)[None, :]
    lv = np.full((CHUNK, CHUNK), -1, np.int32)
    size, level = CHUNK, int(np.log2(CHUNK // BASE))
    while size >= BASE:
        lv = np.where((t // size == s // size) & (s <= t), level, lv)
        size //= 2
        level -= 1
    return lv.astype(np.int32)


N_LEVELS = int(np.log2(CHUNK // BASE)) + 1
A_UNROLL = 8
STATIC_ITERS = 2
C_UNROLL = 8


def _chunk_rows(c):
    if isinstance(c, int):
        return pl.ds(c * CHUNK, CHUNK)
    return pl.ds(pl.multiple_of(c * CHUNK, CHUNK), CHUNK)


def _row(ref, r):
    return jnp.broadcast_to(ref[pl.ds(r, 1), :], (SUBLANES, HEAD))


def _level_factors(level, bc, bc_ref, q, k, fwd):
    groups = CHUNK // SUBLANES
    sub = lax.broadcasted_iota(jnp.int32, (CHUNK, HEAD), 0) % SUBLANES
    if level == 0:
        lo_row, hi_row = (0, BASE) if fwd else (BASE - 1, SUBLANES - 1)
        ref = jnp.concatenate(
            [jnp.where(sub[:SUBLANES] < BASE, _row(bc_ref, g * SUBLANES + lo_row),
                       _row(bc_ref, g * SUBLANES + hi_row)) for g in range(groups)], axis=0)
        e = bc - ref
        return (q * jnp.exp2(e)).astype(BF16), (k * jnp.exp2(-e)).astype(BF16), [slice(0, CHUNK)]
    half = BASE << (level - 1)
    if half < SUBLANES:
        ref_row = half - 1 if fwd else half
        ref = jnp.concatenate([_row(bc_ref, g * SUBLANES + ref_row) for g in range(groups)], axis=0)
        q_rows = (sub >= half) if fwd else (sub < half)
        ex = jnp.exp2(jnp.where(q_rows, bc - ref, ref - bc))
        zero = jnp.zeros_like(ex)
        return (jnp.where(q_rows, q * ex, zero).astype(BF16),
                jnp.where(q_rows, zero, k * ex).astype(BF16), [slice(0, CHUNK)])
    qs, ks, q_rows = [], [], []
    zero = jnp.zeros((half, HEAD), F32)
    for blk in range(CHUNK // (2 * half)):
        a = blk * 2 * half
        lo, hi = slice(a, a + half), slice(a + half, a + 2 * half)
        ref = jnp.broadcast_to(bc_ref[pl.ds(a + half - 1 if fwd else a + half, 1), :], (half, HEAD))
        if fwd:
            qs.append(q[hi] * jnp.exp2(bc[hi] - ref))
            ks += [k[lo] * jnp.exp2(ref - bc[lo]), zero]
            q_rows.append(hi)
        else:
            qs.append(q[lo] * jnp.exp2(bc[lo] - ref))
            ks += [zero, k[hi] * jnp.exp2(ref - bc[hi])]
            q_rows.append(lo)
    return jnp.concatenate(qs, axis=0).astype(BF16), jnp.concatenate(ks, axis=0).astype(BF16), q_rows


def _split2(g):
    hi = g.astype(BF16)
    lo = (g - hi.astype(F32)).astype(BF16)
    return jnp.concatenate([hi, lo], axis=1)


def _rec_kernel(n_ctx, n_lat, ctx_out, *refs):
    (qx, kfx, kbx, lfx, lbx, vx, zx, qc, kfc, kbc, lfc, lbc, vc, zc,
     tril_ref, triu_ref, lvf_ref, lvb_ref, g_ref) = refs[:19]
    if ctx_out:
        ox_ref, oc_ref = refs[19:21]
        scr = refs[21:]
    else:
        ox_ref, oc_ref = refs[19], None
        scr = refs[20:]
    bcf_scr, bcb_scr, qfb_scr, oin_scr, dst_scr, a_scr, st_scr = scr
    kw = HEAD

    def chunks_a(q_ref, kf_ref, kb_ref, lf_ref, lb_ref, v_ref, chunks, slot0, outputs):
        dirs = ((True, kf_ref, lf_ref, tril_ref, lvf_ref, bcf_scr),
                (False, kb_ref, lb_ref, triu_ref, lvb_ref, bcb_scr))
        rows = [_chunk_rows(c) for c in chunks]
        units = [(u, d) for u in range(len(chunks)) for d in range(2)]
        q = [q_ref[0, r, :].astype(F32) for r in rows] if outputs else None
        v = [v_ref[0, r, :] for r in rows]
        k = {(u, d): dirs[d][1][0, rows[u], :].astype(F32) for u, d in units}
        cs = {(u, d): _dot(dirs[d][3][...], _split2(dirs[d][2][0, rows[u], :])) for u, d in units}
        bc, bref = {}, {}
        for u, d in units:
            bc[u, d] = (cs[u, d][:, :kw] + cs[u, d][:, kw:]) * LOG2_E
            bref[u, d] = dirs[d][5].at[u]
            bref[u, d][...] = bc[u, d]
        qi, kd, dec = {}, {}, {}
        for u, d in units:
            b_edge = bref[u, d][pl.ds(CHUNK - 1 if d == 0 else 0, 1), :]
            if outputs:
                qi[u, d] = (q[u] * jnp.exp2(bc[u, d])).astype(BF16)
            kd[u, d] = (k[u, d] * jnp.exp2(b_edge - bc[u, d])).astype(BF16)
            dec[u, d] = jnp.exp2(b_edge)
        groups = CHUNK // SUBLANES
        sc = {ud: [jnp.zeros((SUBLANES, CHUNK), F32) for _ in range(groups)] for ud in units}
        for level in range(N_LEVELS if outputs else 0):
            for u, d in units:
                qt, kt, q_rows = _level_factors(level, bc[u, d], bref[u, d], q[u], k[u, d], d == 0)
                p = _dot_nt(qt, kt)
                at = 0
                for sl in q_rows:
                    for g in range(sl.start // SUBLANES, sl.stop // SUBLANES):
                        own = dirs[d][4][g * SUBLANES:(g + 1) * SUBLANES, :] == level
                        sc[u, d][g] = jnp.where(own, p[at:at + SUBLANES], sc[u, d][g])
                        at += SUBLANES
        for u in range(len(chunks)):
            slot = slot0 + chunks[u]
            if outputs:
                scores = (jnp.concatenate(sc[u, 0], axis=0) + jnp.concatenate(sc[u, 1], axis=0)).astype(BF16)
                oin_scr[slot] = _dot(scores, v[u])
                qfb_scr[slot] = jnp.concatenate([qi[u, 0], qi[u, 1]], axis=1)
            dst_scr[slot] = _dot_tn(v[u], jnp.concatenate([kd[u, 0], kd[u, 1]], axis=1))
            a_scr[slot] = jnp.concatenate([dec[u, 0], dec[u, 1]], axis=1)

    def chunk_c(z_ref, o_ref, c, slot):
        rows = _chunk_rows(c)
        o = oin_scr[slot] + _dot_nt(qfb_scr[slot], st_scr[slot])
        o = o * lax.rsqrt(jnp.mean(o * o, axis=-1, keepdims=True) + EPS)
        o_ref[0, rows, :] = (o * g_ref[...] * z_ref[0, rows, :].astype(F32)).astype(BF16)

    def for_chunks(n, per_iter, fn):
        per_iter = max(u for u in range(1, per_iter + 1) if n % u == 0)

        def body(i, carry):
            fn([i * per_iter + u for u in range(per_iter)])
            return carry

        if n // per_iter <= STATIC_ITERS:
            for i in range(n // per_iter):
                body(i, 0)
        else:
            lax.fori_loop(0, n // per_iter, body, 0)

    for_chunks(n_ctx, A_UNROLL, lambda cs: chunks_a(qc, kfc, kbc, lfc, lbc, vc, cs, 0, ctx_out))
    for_chunks(n_lat, A_UNROLL, lambda cs: chunks_a(qx, kfx, kbx, lfx, lbx, vx, cs, n_ctx, True))

    n_all = n_ctx + n_lat
    fwd_order = list(range(n_all))
    bwd_order = list(range(n_ctx - 1, -1, -1)) + list(range(n_all - 1, n_ctx - 1, -1))
    for order, cols in ((fwd_order, slice(0, kw)), (bwd_order, slice(kw, 2 * kw))):
        s = jnp.zeros((kw, kw), F32)
        for slot in order:
            st_scr[slot, :, cols] = s.astype(BF16)
            s = a_scr[slot, :, cols] * s + dst_scr[slot, :, cols]

    for_chunks(n_lat, C_UNROLL, lambda cs: [chunk_c(zx, ox_ref, c, c + n_ctx) for c in cs])
    if ctx_out:
        for_chunks(n_ctx, C_UNROLL, lambda cs: [chunk_c(zc, oc_ref, c, c) for c in cs])


def _recurrence(lat, ctx, rec_g, ctx_out):
    bsz, t, _ = lat[0].shape
    tc = ctx[0].shape[1]
    n_lat, n_ctx = t // CHUNK, tc // CHUNK
    n_all = n_lat + n_ctx
    lvf = _level_ids()
    tri = np.tril(np.ones((CHUNK, CHUNK), np.float32))
    consts = [jnp.asarray(tri, dtype=BF16), jnp.asarray(tri.T, dtype=BF16),
              jnp.asarray(lvf), jnp.asarray(lvf.T)]
    head = lambda n: pl.BlockSpec((1, n, HEAD), lambda b, h: (b, 0, h))
    const = pl.BlockSpec((CHUNK, CHUNK), lambda b, h: (0, 0))
    in_specs = ([head(t)] * 7 + [head(tc)] * 7 + [const] * 4
                + [pl.BlockSpec((1, HEAD), lambda b, h: (0, h))])
    out_specs = [head(t)]
    out_shape = [jax.ShapeDtypeStruct((bsz, t, E_REC), BF16)]
    if ctx_out:
        out_specs.append(head(tc))
        out_shape.append(jax.ShapeDtypeStruct((bsz, tc, E_REC), BF16))
    scratch = [pltpu.VMEM((A_UNROLL, CHUNK, HEAD), F32), pltpu.VMEM((A_UNROLL, CHUNK, HEAD), F32),
               pltpu.VMEM((n_all, CHUNK, 2 * HEAD), BF16),
               pltpu.VMEM((n_all, CHUNK, HEAD), F32),
               pltpu.VMEM((n_all, HEAD, 2 * HEAD), F32),
               pltpu.VMEM((n_all, 1, 2 * HEAD), F32),
               pltpu.VMEM((n_all, HEAD, 2 * HEAD), BF16)]
    out = pl.pallas_call(
        functools.partial(_rec_kernel, n_ctx, n_lat, ctx_out),
        grid=(bsz, REC_HEADS),
        in_specs=in_specs,
        out_specs=out_specs,
        out_shape=out_shape,
        scratch_shapes=scratch,
        compiler_params=_params("parallel", "parallel"),
        name="recurrence",
    )(*lat, *ctx, *consts, rec_g.reshape(1, E_REC))
    return out if ctx_out else (out[0], None)


def _mix_kernel(has_pos, last, d_model, *refs):
    refs = list(refs)
    a_ref, fb_ref, oc_ref, sg_ref, x_ref = refs[:5]
    refs = refs[5:]
    pos_ref = refs.pop(0) if has_pos else None
    gate_ref, wpa_ref, wpb_ref, wpc_ref, wo_ref = refs[:5]
    refs = refs[5:]
    fg_ref = refs.pop(0) if last else None
    (o_ref,) = refs

    ya = _dot(a_ref[0], wpa_ref[...])
    yb = _dot(fb_ref[0], wpb_ref[...])
    yc = _dot(oc_ref[0], wpc_ref[...])
    d = d_model
    merge = lambda j: _sigmoid(sg_ref[0, :, j * d:(j + 1) * d].astype(F32))
    y = merge(0) * ya + merge(1) * yb + merge(2) * yc
    z = _dot(y.astype(BF16), wo_ref[...])
    x = x_ref[0]
    if has_pos:
        x = x + pos_ref[...]
    xn = x + gate_ref[0] * z
    if last:
        xn = xn * lax.rsqrt(jnp.mean(xn * xn, axis=-1, keepdims=True) + EPS) * fg_ref[...]
    o_ref[0] = xn


def _mix(a, fb, oc, sg, x, pos, gate, wpa, wpb, wpc, wo, final_g, tm):
    bsz, t, d = x.shape
    has_pos = pos is not None
    last = final_g is not None
    tok = lambda w: pl.BlockSpec((1, tm, w), lambda b, i: (b, i, 0))
    full = lambda arr: pl.BlockSpec(arr.shape, lambda b, i: (0,) * arr.ndim)
    in_specs = [tok(E_CONV), tok(E_FOURIER), tok(E_REC), tok(N_BRANCH * d), tok(d)]
    args = [a, fb, oc, sg, x]
    if has_pos:
        in_specs.append(pl.BlockSpec((tm, d), lambda b, i: (i, 0)))
        args.append(pos)
    params = [wpa, wpb, wpc, wo]
    in_specs += [pl.BlockSpec((1, 1, d), lambda b, i: (b, 0, 0))] + [full(p) for p in params]
    args += [gate] + params
    if last:
        in_specs.append(full(final_g))
        args.append(final_g)
    return pl.pallas_call(
        functools.partial(_mix_kernel, has_pos, last, d),
        grid=(bsz, t // tm),
        in_specs=in_specs,
        out_specs=tok(d),
        out_shape=jax.ShapeDtypeStruct((bsz, t, d), F32),
        compiler_params=_params("parallel", "parallel"),
        name="mix",
    )(*args)


def _sincos_2d(rows, cols, d):
    quarter = d // 4
    omega = 1.0 / (10000.0 ** (jnp.arange(quarter, dtype=F32) / quarter))
    er = jnp.arange(rows, dtype=F32)[:, None] * omega
    ec = jnp.arange(cols, dtype=F32)[:, None] * omega
    emb_r = jnp.concatenate([jnp.sin(er), jnp.cos(er)], axis=-1)
    emb_c = jnp.concatenate([jnp.sin(ec), jnp.cos(ec)], axis=-1)
    emb = jnp.concatenate([jnp.broadcast_to(emb_r[:, None, :], (rows, cols, d // 2)),
                           jnp.broadcast_to(emb_c[None, :, :], (rows, cols, d // 2))], axis=-1)
    return emb.reshape(rows * cols, d)


def _tile(n, want):
    return want if n % want == 0 else n


def kernel(x, c, ctx, c_ctx, w_ada, b_ada, norm_g, w_in, conv_w, conv_b, conv_ln_g, conv_ln_b,
           rec_lb, rec_norm_g, w_pa, w_pb, w_pc, w_out, final_g):
    bsz, t, d = x.shape
    tc = ctx.shape[1]
    depth = w_in.shape[0]
    assert t % CHUNK == 0 and tc % CHUNK == 0 and t % GRID_W == 0

    pos = _sincos_2d(t // GRID_W, GRID_W, d)
    lb_soft = jax.nn.softmax(rec_lb.astype(F32), axis=1)
    lbs = jnp.cumsum(lb_soft, axis=1) - lb_soft[:, :1]

    rows = -(-(bsz + 1) // SUBLANES) * SUBLANES
    cc = jnp.zeros((rows, d), F32).at[:bsz].set(c).at[bsz].set(c_ctx)
    mod = _ada(cc, w_ada, b_ada)

    tm_x, tm_c = _tile(t, 256), _tile(tc, 256)
    w_bf = w_in.astype(BF16)
    for l in range(depth):
        last = l == depth - 1
        shift, scale, gate = mod[l, :, :d], mod[l, :, d:2 * d], mod[l, :, 2 * d:]
        mult = norm_g[l][None, :] * (1.0 + scale)
        mult_x, shift_x, gate_x = (a[:bsz, None, :] for a in (mult, shift, gate))
        mult_c, shift_c, gate_c = (jnp.broadcast_to(a[bsz][None, None, :], (bsz, 1, d))
                                   for a in (mult, shift, gate))
        w_l = w_bf[l]
        omlb = 1.0 - lbs[:, l, :]
        conv = (jnp.broadcast_to(conv_w[l].reshape(CONV_WIDTH, 1, E_CONV), (CONV_WIDTH, SUBLANES, E_CONV)),
                conv_b[l][None], conv_ln_g[l][None], conv_ln_b[l][None])

        px = _inproj(x, pos if l == 0 else None, mult_x, shift_x, w_bf, l, omlb, conv, tm_x)
        if last:
            kf_c, kb_c, lf_c, lb_c, v_c = _inproj_states(ctx, mult_c, shift_c, w_l, omlb, tm_c)
            rec_c = (kf_c, kf_c, kb_c, lf_c, lb_c, v_c, kf_c)
        else:
            pc = _inproj(ctx, None, mult_c, shift_c, w_bf, l, omlb, conv, tm_c)
            rec_c = pc[3:10]

        ox, oc = _recurrence(px[3:10], rec_c, rec_norm_g[l], ctx_out=not last)

        mixw = (w_pa[l].astype(BF16), w_pb[l].astype(BF16), w_pc[l].astype(BF16), w_out[l].astype(BF16))
        fbx = _fourier(px[1], px[2], _tile(t, 1024))
        x = _mix(px[0], fbx, ox, px[10], x, pos if l == 0 else None, gate_x, *mixw,
                 final_g[None] if last else None, _tile(t, 512))
        if not last:
            fbc = _fourier(pc[1], pc[2], _tile(tc, 512))
            ctx = _mix(pc[0], fbc, oc, pc[10], ctx, None, gate_c, *mixw, None, tm_c)
    return x
```
